```python
import jax, jax.numpy as jnp
from jax import lax
import numpy as np

D_MODEL = 1024
BATCH = 8
SEQ = 4096
DEPTH = 1

CHUNK = 64
D_MIX = D_MODEL
HEAD_DIM = 64
SGU_HEADS = 8
SGU_WIDTH = SGU_HEADS * HEAD_DIM
SGU_BLOCK = 128
FOX_HEADS = 8
FOX_WIDTH = FOX_HEADS * HEAD_DIM
Q_BLOCK = 128
D_FF = 2816
EPS = 1e-6
N_OUT_GROUPS = SGU_HEADS + FOX_HEADS
SPLITS = (SGU_WIDTH, 2 * SGU_WIDTH, 2 * SGU_WIDTH + FOX_WIDTH,
          2 * SGU_WIDTH + 2 * FOX_WIDTH, 2 * SGU_WIDTH + 3 * FOX_WIDTH)
IN_COLS = 2 * SGU_WIDTH + 3 * FOX_WIDTH + FOX_HEADS

kernel_name = "hybrid_sgu_fox_macaron_block"


def rmsnorm(x, g):
    xf = x.astype(jnp.float32)
    y = xf * lax.rsqrt(jnp.mean(xf * xf, axis=-1, keepdims=True) + EPS)
    return (y * g.astype(jnp.float32)).astype(x.dtype)


def layernorm(x, g, b):
    xf = x.astype(jnp.float32)
    mu = jnp.mean(xf, axis=-1, keepdims=True)
    xc = xf - mu
    y = xc * lax.rsqrt(jnp.mean(xc * xc, axis=-1, keepdims=True) + EPS)
    return (y * g.astype(jnp.float32) + b.astype(jnp.float32)).astype(x.dtype)


def swiglu(h, w1, w3, w2):
    return (jax.nn.silu(h @ w1) * (h @ w3)) @ w2


def spatial_gating(u, v, ln_g, ln_b, w_s, b_s):
    B, S, _ = v.shape
    u = jax.nn.gelu(u)
    v = layernorm(jax.nn.gelu(v), ln_g, ln_b)
    nb = S // SGU_BLOCK
    vb = v.reshape(B, nb, SGU_BLOCK, SGU_HEADS, HEAD_DIM)
    chunk_id = jnp.arange(SGU_BLOCK) // CHUNK
    mask = chunk_id[:, None] >= chunk_id[None, :]
    ws = jnp.where(mask[None], w_s, jnp.zeros_like(w_s))
    mixed = jnp.einsum('hts,bnshc->bnthc', ws.astype(v.dtype), vb)
    mixed = mixed + b_s.T.astype(v.dtype)[None, None, :, :, None]
    return u * mixed.reshape(B, S, SGU_WIDTH)


def forgetting_attention(q, k, v, f_logit, b_f):
    B, S, _ = q.shape
    to_heads = lambda t: t.reshape(B, S, FOX_HEADS, HEAD_DIM).transpose(0, 2, 1, 3)
    q, k, v = to_heads(q), to_heads(k), to_heads(v)
    log_f = jax.nn.log_sigmoid(f_logit.astype(jnp.float32) + b_f.astype(jnp.float32))
    F = jnp.cumsum(log_f, axis=1).transpose(0, 2, 1)
    nq = S // Q_BLOCK
    qb = q.reshape(B, FOX_HEADS, nq, Q_BLOCK, HEAD_DIM).transpose(2, 0, 1, 3, 4)
    Fq = F.reshape(B, FOX_HEADS, nq, Q_BLOCK).transpose(2, 0, 1, 3)
    key_pos = jnp.arange(S)
    scale = HEAD_DIM ** -0.5

    def block(args):
        qi, Fi, i = args
        logits = jnp.einsum('bhqd,bhkd->bhqk', qi, k).astype(jnp.float32) * scale
        logits = logits + Fi[..., None] - F[:, :, None, :]
        qpos = i * Q_BLOCK + jnp.arange(Q_BLOCK)
        allowed = key_pos[None, :] <= qpos[:, None]
        logits = jnp.where(allowed, logits, -jnp.inf)
        p = jax.nn.softmax(logits, axis=-1)
        return jnp.einsum('bhqk,bhkd->bhqd', p.astype(v.dtype), v)

    out = lax.map(block, (qb, Fq, jnp.arange(nq)))
    return out.transpose(1, 0, 3, 2, 4).reshape(B, S, FOX_WIDTH)


def hybrid_mixer(h, w_in, b_f, ln_g, ln_b, w_s, b_s, out_g, w_out):
    B, S, _ = h.shape
    z = h @ w_in
    u, v_s, q, k, v_a, f_logit = jnp.split(z, list(SPLITS), axis=-1)
    y_a = spatial_gating(u, v_s, ln_g, ln_b, w_s, b_s)
    y_b = forgetting_attention(q, k, v_a, f_logit, b_f)
    y = jnp.concatenate([y_a, y_b], axis=-1).reshape(B, S, N_OUT_GROUPS, HEAD_DIM)
    y = rmsnorm(y, jnp.ones((HEAD_DIM,), jnp.float32)).reshape(B, S, D_MIX) * out_g.astype(h.dtype)
    return y @ w_out


def setup_inputs(seed: int = 0) -> dict:
    key = jax.random.key(seed)
    ks = jax.random.split(key, 24)
    n = lambda k, shape, s: jax.random.normal(k, shape, jnp.float32) * s
    L = DEPTH
    return {
        "x": jax.random.normal(ks[0], (BATCH, SEQ, D_MODEL), jnp.float32),
        "ffn1_norm_g": 1.0 + n(ks[1], (L, D_MODEL), 0.02),
        "ffn1_w1": n(ks[2], (L, D_MODEL, D_FF), D_MODEL ** -0.5),
        "ffn1_w3": n(ks[3], (L, D_MODEL, D_FF), D_MODEL ** -0.5),
        "ffn1_w2": n(ks[4], (L, D_FF, D_MODEL), D_FF ** -0.5),
        "mix_norm_g": 1.0 + n(ks[5], (L, D_MODEL), 0.02),
        "w_in": n(ks[6], (L, D_MODEL, IN_COLS), D_MODEL ** -0.5),
        "fox_f_bias": 2.0 + n(ks[7], (L, FOX_HEADS), 0.1),
        "sgu_ln_g": 1.0 + n(ks[8], (L, SGU_WIDTH), 0.02),
        "sgu_ln_b": n(ks[9], (L, SGU_WIDTH), 0.02),
        "sgu_w_s": n(ks[10], (L, SGU_HEADS, SGU_BLOCK, SGU_BLOCK), SGU_BLOCK ** -0.5),
        "sgu_b_s": 1.0 + n(ks[11], (L, SGU_HEADS, SGU_BLOCK), 0.1),
        "mix_out_g": 1.0 + n(ks[12], (L, D_MIX), 0.02),
        "w_out": n(ks[13], (L, D_MIX, D_MODEL), D_MIX ** -0.5),
        "ffn2_norm_g": 1.0 + n(ks[14], (L, D_MODEL), 0.02),
        "ffn2_w1": n(ks[15], (L, D_MODEL, D_FF), D_MODEL ** -0.5),
        "ffn2_w3": n(ks[16], (L, D_MODEL, D_FF), D_MODEL ** -0.5),
        "ffn2_w2": n(ks[17], (L, D_FF, D_MODEL), D_FF ** -0.5),
        "final_norm_g": 1.0 + n(ks[18], (D_MODEL,), 0.02),
    }


def reference(x, ffn1_norm_g, ffn1_w1, ffn1_w3, ffn1_w2, mix_norm_g, w_in, fox_f_bias,
              sgu_ln_g, sgu_ln_b, sgu_w_s, sgu_b_s, mix_out_g, w_out,
              ffn2_norm_g, ffn2_w1, ffn2_w3, ffn2_w2, final_norm_g):
    for l in range(DEPTH):
        x = x + 0.5 * swiglu(rmsnorm(x, ffn1_norm_g[l]), ffn1_w1[l], ffn1_w3[l], ffn1_w2[l])
        x = x + hybrid_mixer(rmsnorm(x, mix_norm_g[l]), w_in[l], fox_f_bias[l], sgu_ln_g[l],
                             sgu_ln_b[l], sgu_w_s[l], sgu_b_s[l], mix_out_g[l], w_out[l])
        x = x + 0.5 * swiglu(rmsnorm(x, ffn2_norm_g[l]), ffn2_w1[l], ffn2_w3[l], ffn2_w2[l])
    return rmsnorm(x, final_norm_g)
```

```python
import functools

import jax
import jax.numpy as jnp
from jax import lax
from jax.experimental import pallas as pl
from jax.experimental.pallas import tpu as pltpu

F32 = jnp.float32
BF16 = jnp.bfloat16

HEAD_DIM = 64
CHUNK = 64
SGU_BLOCK = 128
EPS = 1e-6
LANES = 128
FORGET_PARTS = 3

TM = 512
TQ = 512
TK = 256
FF_CHUNK = 256
NEG = -1e30
VMEM_LIMIT = 58 * 1024 * 1024


def _iota(shape, dim):
    return lax.broadcasted_iota(jnp.int32, shape, dim)


def _rmsnorm(x, g):
    return x * lax.rsqrt(jnp.mean(x * x, axis=-1, keepdims=True) + EPS) * g


def _log_sigmoid(x):
    return jnp.minimum(x, 0.0) - jnp.log1p(jnp.exp(-jnp.abs(x)))


def _swiglu(h, w1_ref, w3_ref, w2_ref, g_scr):
    d_ff = w1_ref.shape[1]
    for c in range(d_ff // FF_CHUNK):
        sl = slice(c * FF_CHUNK, (c + 1) * FF_CHUNK)
        a = jnp.dot(h, w1_ref[:, sl], preferred_element_type=F32)
        b = jnp.dot(h, w3_ref[:, sl], preferred_element_type=F32)
        g_scr[:, sl] = (a * jax.nn.sigmoid(a) * b).astype(BF16)
    return jnp.dot(g_scr[...], w2_ref[...], preferred_element_type=F32)


def _front_kernel(x_ref, g1_ref, w1_ref, w3_ref, w2_ref, gm_ref, wu_ref, wv_ref, wk_ref,
                  wf_ref, wqvt_ref, bf_ref, lng_ref, lnb_ref, ws_ref, bs_ref, og_ref,
                  gmat_ref, x1_ref, ya_ref, k_ref, fa_ref, qt_ref, vt_ref,
                  g_scr, ya_scr, carry_ref, *, tiles_per_seq, scale):
    i = pl.program_id(0)
    tm = x_ref.shape[0]
    width = wu_ref.shape[1]

    x = x_ref[...]
    h = _rmsnorm(x, g1_ref[...]).astype(BF16)
    x1 = x + 0.5 * _swiglu(h, w1_ref, w3_ref, w2_ref, g_scr)
    x1_ref[...] = x1
    h2 = _rmsnorm(x1, gm_ref[...]).astype(BF16)

    zt = lax.dot_general(wqvt_ref[...], h2, (((1,), (1,)), ((), ())),
                         preferred_element_type=F32)
    qt_ref[0] = (zt[:width] * scale).astype(BF16)
    for t in range(tm // TK):
        vt_ref[t] = zt[width:, t * TK:(t + 1) * TK].astype(BF16)
    k_ref[...] = jnp.dot(h2, wk_ref[...], preferred_element_type=F32).astype(BF16)

    zf = jnp.dot(h2, wf_ref[...], preferred_element_type=F32)
    lane = _iota((tm, LANES), 1)
    row = _iota((tm, LANES), 0)
    n_heads = width // HEAD_DIM
    fsum = jnp.where(lane < n_heads, _log_sigmoid(zf + bf_ref[...]), 0.0)
    shift = 1
    while shift < tm:
        fsum = fsum + jnp.where(row >= shift, pltpu.roll(fsum, shift, axis=0), 0.0)
        shift *= 2

    @pl.when(lax.rem(i, tiles_per_seq) == 0)
    def _():
        carry_ref[...] = jnp.zeros_like(carry_ref)

    fsum = fsum + carry_ref[...]
    carry_ref[...] = fsum[tm - 1:tm, :]
    neg = -fsum
    hi = neg.astype(BF16)
    rem = neg - hi.astype(F32)
    mid = rem.astype(BF16)
    lo = (rem - mid.astype(F32)).astype(BF16)
    parts = (hi.astype(F32) + pltpu.roll(mid.astype(F32), n_heads, axis=1)
             + pltpu.roll(lo.astype(F32), 2 * n_heads, axis=1))
    fa_ref[...] = parts.astype(BF16)

    u = jax.nn.gelu(jnp.dot(h2, wu_ref[...], preferred_element_type=F32))
    gv = jax.nn.gelu(jnp.dot(h2, wv_ref[...], preferred_element_type=F32))
    mu = jnp.mean(gv, axis=-1, keepdims=True)
    xc = gv - mu
    var = jnp.mean(xc * xc, axis=-1, keepdims=True)
    vn = (xc * lax.rsqrt(var + EPS) * lng_ref[...] + lnb_ref[...]).astype(BF16)

    pr = jnp.bitwise_and(_iota((2 * SGU_BLOCK, SGU_BLOCK), 0), SGU_BLOCK - 1)
    pc = _iota((2 * SGU_BLOCK, SGU_BLOCK), 1)
    keep = (pr // CHUNK) >= (pc // CHUNK)
    first_head = _iota((SGU_BLOCK, LANES), 1) < HEAD_DIM
    for p in range(width // LANES):
        ls = slice(p * LANES, (p + 1) * LANES)
        wm = jnp.where(keep, ws_ref[p], jnp.zeros_like(ws_ref[p]))
        for blk in range(tm // SGU_BLOCK):
            rs = slice(blk * SGU_BLOCK, (blk + 1) * SGU_BLOCK)
            r = jnp.dot(wm, vn[rs, ls], preferred_element_type=F32)
            mixed = jnp.where(first_head, r[:SGU_BLOCK], r[SGU_BLOCK:]) + bs_ref[:, ls]
            ya_scr[rs, ls] = u[rs, ls] * mixed

    ya = ya_scr[...]
    sq = (ya * ya).astype(BF16)
    gw = gmat_ref.shape[0]
    ms = jnp.concatenate(
        [jnp.dot(sq[:, c * gw:(c + 1) * gw], gmat_ref[...], preferred_element_type=F32)
         for c in range(width // gw)], axis=1)
    ya_ref[...] = (ya * lax.rsqrt(ms + EPS) * og_ref[...]).astype(BF16)


def _fox_kernel(qt_ref, k_ref, fa_ref, vt_ref, og_ref, yb_ref, o_scr, *, n_heads):
    pair = pl.program_id(1)
    qi = pl.program_id(2)
    tq = qt_ref.shape[1]
    qt = qt_ref[...]
    rows = _iota((LANES, tq), 0)
    diag_steps = tq // TK
    n_full = qi * diag_steps

    for hh in range(LANES // HEAD_DIM):
        head = pair * (LANES // HEAD_DIM) + hh
        qh = jnp.where((rows // HEAD_DIM) == hh, qt, jnp.zeros_like(qt))
        ones = jnp.where((rows < FORGET_PARTS * n_heads) & (lax.rem(rows, n_heads) == head),
                         1.0, 0.0).astype(BF16)
        qaug = jnp.concatenate([qh, ones], axis=0)

        def step(j, carry, masked, qaug=qaug, hh=hh):
            m, l, acc = carry
            k0 = pl.multiple_of(j * TK, TK)
            kb = jnp.concatenate([k_ref[pl.ds(k0, TK), :], fa_ref[pl.ds(k0, TK), :]], axis=1)
            s = jnp.dot(kb, qaug, preferred_element_type=F32)
            if masked:
                spos = j * TK + _iota((TK, tq), 0)
                tpos = qi * tq + _iota((TK, tq), 1)
                s = jnp.where(spos <= tpos, s, NEG)
            m_new = jnp.maximum(m, jnp.max(s, axis=0, keepdims=True))
            alpha = jnp.exp(m - m_new)
            p = jnp.exp(s - m_new)
            l = alpha * l + jnp.sum(p, axis=0, keepdims=True)
            pv = jnp.dot(vt_ref[j, hh * HEAD_DIM:(hh + 1) * HEAD_DIM, :], p.astype(BF16),
                         preferred_element_type=F32)
            return m_new, l, alpha * acc + pv

        carry = (jnp.full((1, tq), NEG, F32), jnp.zeros((1, tq), F32),
                 jnp.zeros((HEAD_DIM, tq), F32))
        carry = lax.fori_loop(0, n_full, functools.partial(step, masked=False), carry)
        for d in range(diag_steps):
            carry = step(n_full + d, carry, True)
        _, l, acc = carry
        o = acc / l
        o = o * lax.rsqrt(jnp.mean(o * o, axis=0, keepdims=True) + EPS)
        o_scr[hh * HEAD_DIM:(hh + 1) * HEAD_DIM, :] = o

    yb_ref[...] = (o_scr[...].T * og_ref[...]).astype(BF16)


def _back_kernel(x1_ref, ya_ref, yb_ref, wo_ref, g2_ref, w1_ref, w3_ref, w2_ref, gf_ref,
                 out_ref, g_scr, *, final_norm):
    y = jnp.concatenate([ya_ref[...], yb_ref[...]], axis=1)
    x2 = x1_ref[...] + jnp.dot(y, wo_ref[...], preferred_element_type=F32)
    h = _rmsnorm(x2, g2_ref[...]).astype(BF16)
    x3 = x2 + 0.5 * _swiglu(h, w1_ref, w3_ref, w2_ref, g_scr)
    out_ref[...] = _rmsnorm(x3, gf_ref[...]) if final_norm else x3


def _resident(shape):
    return pl.BlockSpec(shape, lambda *_: (0,) * len(shape), pipeline_mode=pl.Buffered(1))


def _row(v):
    return v.reshape(1, -1).astype(F32)


def _layer(xt, seq, final_norm, g1, w1a, w3a, w2a, gm, w_in, b_f, ln_g, ln_b, w_s, b_s, out_g,
           w_out, g2, w1b, w3b, w2b, gf):
    t_total, d = xt.shape
    d_ff = w1a.shape[1]
    n_heads = b_f.shape[0]
    width = n_heads * HEAD_DIM
    assert w_s.shape == (n_heads, SGU_BLOCK, SGU_BLOCK) and ln_g.shape == (width,)
    assert w_in.shape == (d, 5 * width + n_heads) and FORGET_PARTS * n_heads <= LANES
    assert seq % TQ == 0 and seq % TM == 0 and TQ % TK == 0 and TM % TK == 0
    assert TM % SGU_BLOCK == 0 and d_ff % FF_CHUNK == 0 and width % (2 * LANES) == 0

    bf = lambda a: a.astype(BF16)
    wu, wv = bf(w_in[:, :width]), bf(w_in[:, width:2 * width])
    wq, wk = w_in[:, 2 * width:3 * width], bf(w_in[:, 3 * width:4 * width])
    wva = w_in[:, 4 * width:5 * width]
    wf = bf(jnp.pad(w_in[:, 5 * width:], ((0, 0), (0, LANES - n_heads))))
    wqvt = bf(jnp.concatenate([wq, wva], axis=1).T)
    bf_row = jnp.pad(_row(b_f), ((0, 0), (0, LANES - n_heads)))
    ws_pairs = bf(w_s.reshape(n_heads // 2, 2 * SGU_BLOCK, SGU_BLOCK))
    bs_full = jnp.repeat(b_s.T.astype(F32), HEAD_DIM, axis=1)
    gw = 2 * LANES
    gmat = bf(jnp.where((jnp.arange(gw)[:, None] // HEAD_DIM) == (jnp.arange(gw)[None, :] // HEAD_DIM),
                        1.0 / HEAD_DIM, 0.0))
    og = _row(out_g)

    n_tiles = t_total // TM
    tile = lambda cols: pl.BlockSpec((TM, cols), lambda i: (i, 0))
    front = pl.pallas_call(
        functools.partial(_front_kernel, tiles_per_seq=seq // TM, scale=HEAD_DIM ** -0.5),
        name="front",
        grid=(n_tiles,),
        in_specs=[tile(d), _resident((1, d)), _resident((d, d_ff)), _resident((d, d_ff)),
                  _resident((d_ff, d)), _resident((1, d)), _resident((d, width)),
                  _resident((d, width)), _resident((d, width)), _resident((d, LANES)),
                  _resident((2 * width, d)), _resident((1, LANES)), _resident((1, width)),
                  _resident((1, width)), _resident(ws_pairs.shape), _resident((SGU_BLOCK, width)),
                  _resident((1, width)), _resident((gw, gw))],
        out_specs=[tile(d), tile(width), tile(width), tile(LANES),
                   pl.BlockSpec((1, width, TM), lambda i: (i, 0, 0)),
                   pl.BlockSpec((TM // TK, width, TK), lambda i: (i, 0, 0))],
        out_shape=[jax.ShapeDtypeStruct((t_total, d), F32),
                   jax.ShapeDtypeStruct((t_total, width), BF16),
                   jax.ShapeDtypeStruct((t_total, width), BF16),
                   jax.ShapeDtypeStruct((t_total, LANES), BF16),
                   jax.ShapeDtypeStruct((t_total // TM, width, TM), BF16),
                   jax.ShapeDtypeStruct((t_total // TK, width, TK), BF16)],
        scratch_shapes=[pltpu.VMEM((TM, d_ff), BF16), pltpu.VMEM((TM, width), F32),
                        pltpu.VMEM((1, LANES), F32)],
        compiler_params=pltpu.CompilerParams(dimension_semantics=("arbitrary",),
                                             vmem_limit_bytes=VMEM_LIMIT),
    )
    x1, ya, k, fa, qt, vt = front(
        xt, _row(g1), bf(w1a), bf(w3a), bf(w2a), _row(gm), wu, wv, wk, wf, wqvt, bf_row,
        _row(ln_g), _row(ln_b), ws_pairs, bs_full, og[:, :width], gmat)

    n_batch = t_total // seq
    nq = seq // TQ
    fox = pl.pallas_call(
        functools.partial(_fox_kernel, n_heads=n_heads),
        name="fox",
        grid=(n_batch, width // LANES, nq),
        in_specs=[pl.BlockSpec((None, LANES, TQ), lambda b, p, q: (b * nq + q, p, 0)),
                  pl.BlockSpec((seq, LANES), lambda b, p, q: (b, p)),
                  pl.BlockSpec((seq, LANES), lambda b, p, q: (b, 0)),
                  pl.BlockSpec((seq // TK, LANES, TK), lambda b, p, q: (b, p, 0)),
                  pl.BlockSpec((1, LANES), lambda b, p, q: (0, p))],
        out_specs=pl.BlockSpec((TQ, LANES), lambda b, p, q: (b * nq + q, p)),
        out_shape=jax.ShapeDtypeStruct((t_total, width), BF16),
        scratch_shapes=[pltpu.VMEM((LANES, TQ), F32)],
        compiler_params=pltpu.CompilerParams(
            dimension_semantics=("arbitrary", "arbitrary", "arbitrary"),
            vmem_limit_bytes=VMEM_LIMIT),
    )
    yb = fox(qt, k, fa, vt, og[:, width:])

    back = pl.pallas_call(
        functools.partial(_back_kernel, final_norm=final_norm),
        name="back",
        grid=(n_tiles,),
        in_specs=[tile(d), tile(width), tile(width), _resident((2 * width, d)), _resident((1, d)),
                  _resident((d, d_ff)), _resident((d, d_ff)), _resident((d_ff, d)),
                  _resident((1, d))],
        out_specs=tile(d),
        out_shape=jax.ShapeDtypeStruct((t_total, d), F32),
        scratch_shapes=[pltpu.VMEM((TM, d_ff), BF16)],
        compiler_params=pltpu.CompilerParams(dimension_semantics=("arbitrary",),
                                             vmem_limit_bytes=VMEM_LIMIT),
    )
    return back(x1, ya, yb, bf(w_out), _row(g2), bf(w1b), bf(w3b), bf(w2b), _row(gf))


def kernel(x, ffn1_norm_g, ffn1_w1, ffn1_w3, ffn1_w2, mix_norm_g, w_in, fox_f_bias, sgu_ln_g, sgu_ln_b, sgu_w_s, sgu_b_s, mix_out_g, w_out, ffn2_norm_g, ffn2_w1, ffn2_w3, ffn2_w2, final_norm_g):
    n_batch, seq, d = x.shape
    depth = ffn1_w1.shape[0]
    xt = x.reshape(n_batch * seq, d)
    for l in range(depth):
        xt = _layer(xt, seq, l == depth - 1, ffn1_norm_g[l], ffn1_w1[l], ffn1_w3[l], ffn1_w2[l],
                    mix_norm_g[l], w_in[l], fox_f_bias[l], sgu_ln_g[l], sgu_ln_b[l], sgu_w_s[l],
                    sgu_b_s[l], mix_out_g[l], w_out[l], ffn2_norm_g[l], ffn2_w1[l], ffn2_w3[l],
                    ffn2_w2[l], final_norm_g)
    return xt.reshape(n_batch, seq, d)
```

```python
import functools

import jax
import jax.numpy as jnp
from jax import lax
from jax.experimental import pallas as pl
from jax.experimental.pallas import tpu as pltpu

F32 = jnp.float32
BF16 = jnp.bfloat16

HEAD_DIM = 64
CHUNK = 64
SGU_BLOCK = 128
EPS = 1e-6
LANES = 128
FORGET_PARTS = 3

TM = 512
TQ = 512
TK = 256
FF_CHUNK = 256
QK_LOOKAHEAD = 3
NEG = -1e30
VMEM_LIMIT = 58 * 1024 * 1024


def _iota(shape, dim):
    return lax.broadcasted_iota(jnp.int32, shape, dim)


def _rmsnorm(x, g):
    return x * lax.rsqrt(jnp.mean(x * x, axis=-1, keepdims=True) + EPS) * g


def _log_sigmoid(x):
    return jnp.minimum(x, 0.0) - jnp.log1p(jnp.exp(-jnp.abs(x)))


def _swiglu(h, w1_ref, w3_ref, w2_ref, g_scr):
    d_ff = w1_ref.shape[1]
    for c in range(d_ff // FF_CHUNK):
        sl = slice(c * FF_CHUNK, (c + 1) * FF_CHUNK)
        a = jnp.dot(h, w1_ref[:, sl], preferred_element_type=F32)
        b = jnp.dot(h, w3_ref[:, sl], preferred_element_type=F32)
        g_scr[:, sl] = (a * jax.nn.sigmoid(a) * b).astype(BF16)
    return jnp.dot(g_scr[...], w2_ref[...], preferred_element_type=F32)


def _front_kernel(x_ref, g1_ref, w1_ref, w3_ref, w2_ref, gm_ref, wu_ref, wv_ref, wk_ref,
                  wf_ref, wqvt_ref, bf_ref, lng_ref, lnb_ref, ws_ref, bs_ref, og_ref,
                  gmat_ref, x1_ref, ya_ref, k_ref, fa_ref, qt_ref, vt_ref,
                  g_scr, ya_scr, carry_ref, *, tiles_per_seq, scale):
    i = pl.program_id(0)
    tm = x_ref.shape[0]
    width = wu_ref.shape[1]

    x = x_ref[...]
    h = _rmsnorm(x, g1_ref[...]).astype(BF16)
    x1 = x + 0.5 * _swiglu(h, w1_ref, w3_ref, w2_ref, g_scr)
    x1_ref[...] = x1
    h2 = _rmsnorm(x1, gm_ref[...]).astype(BF16)

    zt = lax.dot_general(wqvt_ref[...], h2, (((1,), (1,)), ((), ())),
                         preferred_element_type=F32)
    qt_ref[0] = (zt[:width] * scale).astype(BF16)
    for t in range(tm // TK):
        vt_ref[t] = zt[width:, t * TK:(t + 1) * TK].astype(BF16)
    k_ref[...] = jnp.dot(h2, wk_ref[...], preferred_element_type=F32).astype(BF16)

    zf = jnp.dot(h2, wf_ref[...], preferred_element_type=F32)
    lane = _iota((tm, LANES), 1)
    row = _iota((tm, LANES), 0)
    n_heads = width // HEAD_DIM
    fsum = jnp.where(lane < n_heads, _log_sigmoid(zf + bf_ref[...]), 0.0)
    shift = 1
    while shift < tm:
        fsum = fsum + jnp.where(row >= shift, pltpu.roll(fsum, shift, axis=0), 0.0)
        shift *= 2

    @pl.when(lax.rem(i, tiles_per_seq) == 0)
    def _():
        carry_ref[...] = jnp.zeros_like(carry_ref)

    fsum = fsum + carry_ref[...]
    carry_ref[...] = fsum[tm - 1:tm, :]
    neg = -fsum
    hi = neg.astype(BF16)
    rem = neg - hi.astype(F32)
    mid = rem.astype(BF16)
    lo = (rem - mid.astype(F32)).astype(BF16)
    parts = (hi.astype(F32) + pltpu.roll(mid.astype(F32), n_heads, axis=1)
             + pltpu.roll(lo.astype(F32), 2 * n_heads, axis=1))
    fa_ref[...] = parts.astype(BF16)

    u = jax.nn.gelu(jnp.dot(h2, wu_ref[...], preferred_element_type=F32))
    gv = jax.nn.gelu(jnp.dot(h2, wv_ref[...], preferred_element_type=F32))
    mu = jnp.mean(gv, axis=-1, keepdims=True)
    xc = gv - mu
    var = jnp.mean(xc * xc, axis=-1, keepdims=True)
    vn = (xc * lax.rsqrt(var + EPS) * lng_ref[...] + lnb_ref[...]).astype(BF16)

    pr = jnp.bitwise_and(_iota((2 * SGU_BLOCK, SGU_BLOCK), 0), SGU_BLOCK - 1)
    pc = _iota((2 * SGU_BLOCK, SGU_BLOCK), 1)
    keep = (pr // CHUNK) >= (pc // CHUNK)
    first_head = _iota((SGU_BLOCK, LANES), 1) < HEAD_DIM
    for p in range(width // LANES):
        ls = slice(p * LANES, (p + 1) * LANES)
        wm = jnp.where(keep, ws_ref[p], jnp.zeros_like(ws_ref[p]))
        for blk in range(tm // SGU_BLOCK):
            rs = slice(blk * SGU_BLOCK, (blk + 1) * SGU_BLOCK)
            r = jnp.dot(wm, vn[rs, ls], preferred_element_type=F32)
            mixed = jnp.where(first_head, r[:SGU_BLOCK], r[SGU_BLOCK:]) + bs_ref[:, ls]
            ya_scr[rs, ls] = u[rs, ls] * mixed

    ya = ya_scr[...]
    sq = (ya * ya).astype(BF16)
    gw = gmat_ref.shape[0]
    ms = jnp.concatenate(
        [jnp.dot(sq[:, c * gw:(c + 1) * gw], gmat_ref[...], preferred_element_type=F32)
         for c in range(width // gw)], axis=1)
    ya_ref[...] = (ya * lax.rsqrt(ms + EPS) * og_ref[...]).astype(BF16)


def _fox_kernel(qt_ref, k_ref, fa_ref, vt_ref, og_ref, yb_ref,
                qaug_scr, m_scr, l_scr, acc_scr, *, n_heads):
    qi = pl.program_id(1)
    tq = qt_ref.shape[1]
    diag_steps = tq // TK
    n_full = qi * diag_steps
    heads_per_group = LANES // HEAD_DIM

    rows = _iota((LANES, tq), 0)
    for h in range(n_heads):
        g, hh = divmod(h, heads_per_group)
        qt = qt_ref[g * LANES:(g + 1) * LANES, :]
        qaug_scr[h, :LANES, :] = jnp.where((rows // HEAD_DIM) == hh, qt, jnp.zeros_like(qt))
        qaug_scr[h, LANES:, :] = jnp.where(
            (rows < FORGET_PARTS * n_heads) & (lax.rem(rows, n_heads) == h), 1.0, 0.0).astype(BF16)
    m_scr[...] = jnp.full(m_scr.shape, NEG, F32)
    l_scr[...] = jnp.zeros(l_scr.shape, F32)
    acc_scr[...] = jnp.zeros(acc_scr.shape, F32)

    def step(j, masked):
        k0 = pl.multiple_of(j * TK, TK)
        fblk = fa_ref[pl.ds(k0, TK), :]
        if masked:
            valid = (j * TK + _iota((TK, tq), 0)) <= (qi * tq + _iota((TK, tq), 1))
        def logits(h):
            g = h // heads_per_group
            kb = jnp.concatenate([k_ref[pl.ds(k0, TK), g * LANES:(g + 1) * LANES], fblk], axis=1)
            return jnp.dot(kb, qaug_scr[h], preferred_element_type=F32)

        pending = [logits(h) for h in range(min(QK_LOOKAHEAD, n_heads))]
        for h in range(n_heads):
            if h + QK_LOOKAHEAD < n_heads:
                pending.append(logits(h + QK_LOOKAHEAD))
            s = pending.pop(0)
            if masked:
                s = jnp.where(valid, s, NEG)
            m = m_scr[h]
            m_new = jnp.maximum(m, jnp.max(s, axis=0, keepdims=True))
            alpha = jnp.exp(m - m_new)
            p = jnp.exp(s - m_new)
            m_scr[h] = m_new
            l_scr[h] = alpha * l_scr[h] + jnp.sum(p, axis=0, keepdims=True)
            pv = jnp.dot(vt_ref[j, h * HEAD_DIM:(h + 1) * HEAD_DIM, :], p.astype(BF16),
                         preferred_element_type=F32)
            acc_scr[h] = alpha * acc_scr[h] + pv

    def body(j, carry):
        step(j, False)
        return carry

    lax.fori_loop(0, n_full, body, 0)
    for d in range(diag_steps):
        step(n_full + d, True)

    for h in range(n_heads):
        o = acc_scr[h] / l_scr[h]
        acc_scr[h] = o * lax.rsqrt(jnp.mean(o * o, axis=0, keepdims=True) + EPS)
    o_all = acc_scr[...].reshape(n_heads * HEAD_DIM, tq)
    yb_ref[...] = (o_all.T * og_ref[...]).astype(BF16)


def _back_kernel(x1_ref, ya_ref, yb_ref, wo_ref, g2_ref, w1_ref, w3_ref, w2_ref, gf_ref,
                 out_ref, g_scr, *, final_norm):
    y = jnp.concatenate([ya_ref[...], yb_ref[...]], axis=1)
    x2 = x1_ref[...] + jnp.dot(y, wo_ref[...], preferred_element_type=F32)
    h = _rmsnorm(x2, g2_ref[...]).astype(BF16)
    x3 = x2 + 0.5 * _swiglu(h, w1_ref, w3_ref, w2_ref, g_scr)
    out_ref[...] = _rmsnorm(x3, gf_ref[...]) if final_norm else x3


def _resident(shape):
    return pl.BlockSpec(shape, lambda *_: (0,) * len(shape), pipeline_mode=pl.Buffered(1))


def _row(v):
    return v.reshape(1, -1).astype(F32)


def _layer(xt, seq, final_norm, g1, w1a, w3a, w2a, gm, w_in, b_f, ln_g, ln_b, w_s, b_s, out_g,
           w_out, g2, w1b, w3b, w2b, gf):
    t_total, d = xt.shape
    d_ff = w1a.shape[1]
    n_heads = b_f.shape[0]
    width = n_heads * HEAD_DIM
    assert w_s.shape == (n_heads, SGU_BLOCK, SGU_BLOCK) and ln_g.shape == (width,)
    assert w_in.shape == (d, 5 * width + n_heads) and FORGET_PARTS * n_heads <= LANES
    assert seq % TQ == 0 and seq % TM == 0 and TQ % TK == 0 and TM % TK == 0
    assert TM % SGU_BLOCK == 0 and d_ff % FF_CHUNK == 0 and width % (2 * LANES) == 0

    bf = lambda a: a.astype(BF16)
    wu, wv = bf(w_in[:, :width]), bf(w_in[:, width:2 * width])
    wq, wk = w_in[:, 2 * width:3 * width], bf(w_in[:, 3 * width:4 * width])
    wva = w_in[:, 4 * width:5 * width]
    wf = bf(jnp.pad(w_in[:, 5 * width:], ((0, 0), (0, LANES - n_heads))))
    wqvt = bf(jnp.concatenate([wq, wva], axis=1).T)
    bf_row = jnp.pad(_row(b_f), ((0, 0), (0, LANES - n_heads)))
    ws_pairs = bf(w_s.reshape(n_heads // 2, 2 * SGU_BLOCK, SGU_BLOCK))
    bs_full = jnp.repeat(b_s.T.astype(F32), HEAD_DIM, axis=1)
    gw = 2 * LANES
    gmat = bf(jnp.where((jnp.arange(gw)[:, None] // HEAD_DIM) == (jnp.arange(gw)[None, :] // HEAD_DIM),
                        1.0 / HEAD_DIM, 0.0))
    og = _row(out_g)

    n_tiles = t_total // TM
    tile = lambda cols: pl.BlockSpec((TM, cols), lambda i: (i, 0))
    front = pl.pallas_call(
        functools.partial(_front_kernel, tiles_per_seq=seq // TM, scale=HEAD_DIM ** -0.5),
        name="front",
        grid=(n_tiles,),
        in_specs=[tile(d), _resident((1, d)), _resident((d, d_ff)), _resident((d, d_ff)),
                  _resident((d_ff, d)), _resident((1, d)), _resident((d, width)),
                  _resident((d, width)), _resident((d, width)), _resident((d, LANES)),
                  _resident((2 * width, d)), _resident((1, LANES)), _resident((1, width)),
                  _resident((1, width)), _resident(ws_pairs.shape), _resident((SGU_BLOCK, width)),
                  _resident((1, width)), _resident((gw, gw))],
        out_specs=[tile(d), tile(width), tile(width), tile(LANES),
                   pl.BlockSpec((1, width, TM), lambda i: (i, 0, 0)),
                   pl.BlockSpec((TM // TK, width, TK), lambda i: (i, 0, 0))],
        out_shape=[jax.ShapeDtypeStruct((t_total, d), F32),
                   jax.ShapeDtypeStruct((t_total, width), BF16),
                   jax.ShapeDtypeStruct((t_total, width), BF16),
                   jax.ShapeDtypeStruct((t_total, LANES), BF16),
                   jax.ShapeDtypeStruct((t_total // TM, width, TM), BF16),
                   jax.ShapeDtypeStruct((t_total // TK, width, TK), BF16)],
        scratch_shapes=[pltpu.VMEM((TM, d_ff), BF16), pltpu.VMEM((TM, width), F32),
                        pltpu.VMEM((1, LANES), F32)],
        compiler_params=pltpu.CompilerParams(dimension_semantics=("arbitrary",),
                                             vmem_limit_bytes=VMEM_LIMIT),
    )
    x1, ya, k, fa, qt, vt = front(
        xt, _row(g1), bf(w1a), bf(w3a), bf(w2a), _row(gm), wu, wv, wk, wf, wqvt, bf_row,
        _row(ln_g), _row(ln_b), ws_pairs, bs_full, og[:, :width], gmat)

    n_batch = t_total // seq
    nq = seq // TQ
    fox = pl.pallas_call(
        functools.partial(_fox_kernel, n_heads=n_heads),
        name="fox",
        grid=(n_batch, nq),
        in_specs=[pl.BlockSpec((None, width, TQ), lambda b, q: (b * nq + q, 0, 0)),
                  pl.BlockSpec((seq, width), lambda b, q: (b, 0)),
                  pl.BlockSpec((seq, LANES), lambda b, q: (b, 0)),
                  pl.BlockSpec((seq // TK, width, TK), lambda b, q: (b, 0, 0)),
                  pl.BlockSpec((1, width), lambda b, q: (0, 0))],
        out_specs=pl.BlockSpec((TQ, width), lambda b, q: (b * nq + q, 0)),
        out_shape=jax.ShapeDtypeStruct((t_total, width), BF16),
        scratch_shapes=[pltpu.VMEM((n_heads, 2 * LANES, TQ), BF16),
                        pltpu.VMEM((n_heads, 1, TQ), F32), pltpu.VMEM((n_heads, 1, TQ), F32),
                        pltpu.VMEM((n_heads, HEAD_DIM, TQ), F32)],
        compiler_params=pltpu.CompilerParams(
            dimension_semantics=("arbitrary", "arbitrary"),
            vmem_limit_bytes=VMEM_LIMIT),
    )
    yb = fox(qt, k, fa, vt, og[:, width:])

    back = pl.pallas_call(
        functools.partial(_back_kernel, final_norm=final_norm),
        name="back",
        grid=(n_tiles,),
        in_specs=[tile(d), tile(width), tile(width), _resident((2 * width, d)), _resident((1, d)),
                  _resident((d, d_ff)), _resident((d, d_ff)), _resident((d_ff, d)),
                  _resident((1, d))],
        out_specs=tile(d),
        out_shape=jax.ShapeDtypeStruct((t_total, d), F32),
        scratch_shapes=[pltpu.VMEM((TM, d_ff), BF16)],
        compiler_params=pltpu.CompilerParams(dimension_semantics=("arbitrary",),
                                             vmem_limit_bytes=VMEM_LIMIT),
    )
    return back(x1, ya, yb, bf(w_out), _row(g2), bf(w1b), bf(w3b), bf(w2b), _row(gf))


def kernel(x, ffn1_norm_g, ffn1_w1, ffn1_w3, ffn1_w2, mix_norm_g, w_in, fox_f_bias, sgu_ln_g, sgu_ln_b, sgu_w_s, sgu_b_s, mix_out_g, w_out, ffn2_norm_g, ffn2_w1, ffn2_w3, ffn2_w2, final_norm_g):
    n_batch, seq, d = x.shape
    depth = ffn1_w1.shape[0]
    xt = x.reshape(n_batch * seq, d)
    for l in range(depth):
        xt = _layer(xt, seq, l == depth - 1, ffn1_norm_g[l], ffn1_w1[l], ffn1_w3[l], ffn1_w2[l],
                    mix_norm_g[l], w_in[l], fox_f_bias[l], sgu_ln_g[l], sgu_ln_b[l], sgu_w_s[l],
                    sgu_b_s[l], mix_out_g[l], w_out[l], ffn2_norm_g[l], ffn2_w1[l], ffn2_w3[l],
                    ffn2_w2[l], final_norm_g)
    return xt.reshape(n_batch, seq, d)
```

```python
import functools

import jax
import jax.numpy as jnp
from jax import lax
from jax.experimental import pallas as pl
from jax.experimental.pallas import tpu as pltpu

F32 = jnp.float32
BF16 = jnp.bfloat16

HEAD_DIM = 64
CHUNK = 64
SGU_BLOCK = 128
EPS = 1e-6
LANES = 128
FORGET_PARTS = 3

TM = 512
SUBLANES_BF16 = 16
LOG2E = 1.4426950408889634

TQ = 512
TQS = 256
TK = 128
KV_UNROLL = 4
FF_CHUNK = 256
QK_LOOKAHEAD = 6
EXP_DELAY = 4
STAGE_RING = 8
NEG = -1e30
VMEM_LIMIT = 58 * 1024 * 1024


def _iota(shape, dim):
    return lax.broadcasted_iota(jnp.int32, shape, dim)


def _rmsnorm(x, g):
    return x * lax.rsqrt(jnp.mean(x * x, axis=-1, keepdims=True) + EPS) * g


def _log_sigmoid(x):
    return jnp.minimum(x, 0.0) - jnp.log1p(jnp.exp(-jnp.abs(x)))


def _swiglu(h, w1_ref, w3_ref, w2_ref, g_scr):
    d_ff = w1_ref.shape[1]
    for c in range(d_ff // FF_CHUNK):
        sl = slice(c * FF_CHUNK, (c + 1) * FF_CHUNK)
        a = jnp.dot(h, w1_ref[:, sl], preferred_element_type=F32)
        b = jnp.dot(h, w3_ref[:, sl], preferred_element_type=F32)
        g_scr[:, sl] = (a * jax.nn.sigmoid(a) * b).astype(BF16)
    return jnp.dot(g_scr[...], w2_ref[...], preferred_element_type=F32)


def _front_kernel(x_ref, g1_ref, w1_ref, w3_ref, w2_ref, gm_ref, wu_ref, wv_ref, wk_ref,
                  wf_ref, wqvt_ref, bf_ref, lng_ref, lnb_ref, ws_ref, bs_ref, og_ref,
                  gmat_ref, x1_ref, ya_ref, k_ref, fa_ref, qt_ref, vt_ref,
                  g_scr, ya_scr, carry_ref, *, tiles_per_seq, scale):
    i = pl.program_id(0)
    tm = x_ref.shape[0]
    width = wu_ref.shape[1]

    x = x_ref[...]
    h = _rmsnorm(x, g1_ref[...]).astype(BF16)
    x1 = x + 0.5 * _swiglu(h, w1_ref, w3_ref, w2_ref, g_scr)
    x1_ref[...] = x1
    h2 = _rmsnorm(x1, gm_ref[...]).astype(BF16)

    zt = lax.dot_general(wqvt_ref[...], h2, (((1,), (1,)), ((), ())),
                         preferred_element_type=F32)
    qt_ref[0] = (zt[:width] * scale).astype(BF16)
    for t in range(tm // TK):
        vt_ref[t] = zt[width:, t * TK:(t + 1) * TK].astype(BF16)
    k_ref[...] = jnp.dot(h2, wk_ref[...], preferred_element_type=F32).astype(BF16)

    zf = jnp.dot(h2, wf_ref[...], preferred_element_type=F32)
    lane = _iota((tm, LANES), 1)
    row = _iota((tm, LANES), 0)
    n_heads = width // HEAD_DIM
    fsum = jnp.where(lane < n_heads, _log_sigmoid(zf + bf_ref[...]), 0.0)
    shift = 1
    while shift < tm:
        fsum = fsum + jnp.where(row >= shift, pltpu.roll(fsum, shift, axis=0), 0.0)
        shift *= 2

    @pl.when(lax.rem(i, tiles_per_seq) == 0)
    def _():
        carry_ref[...] = jnp.zeros_like(carry_ref)

    fsum = fsum + carry_ref[...]
    carry_ref[...] = fsum[tm - 1:tm, :]
    neg = fsum * (-LOG2E)
    hi = neg.astype(BF16)
    rem = neg - hi.astype(F32)
    mid = rem.astype(BF16)
    lo = (rem - mid.astype(F32)).astype(BF16)
    parts = (hi.astype(F32) + pltpu.roll(mid.astype(F32), n_heads, axis=1)
             + pltpu.roll(lo.astype(F32), 2 * n_heads, axis=1))
    fa_ref[...] = parts.astype(BF16)

    u = jax.nn.gelu(jnp.dot(h2, wu_ref[...], preferred_element_type=F32))
    gv = jax.nn.gelu(jnp.dot(h2, wv_ref[...], preferred_element_type=F32))
    mu = jnp.mean(gv, axis=-1, keepdims=True)
    xc = gv - mu
    var = jnp.mean(xc * xc, axis=-1, keepdims=True)
    vn = (xc * lax.rsqrt(var + EPS) * lng_ref[...] + lnb_ref[...]).astype(BF16)

    pr = jnp.bitwise_and(_iota((2 * SGU_BLOCK, SGU_BLOCK), 0), SGU_BLOCK - 1)
    pc = _iota((2 * SGU_BLOCK, SGU_BLOCK), 1)
    keep = (pr // CHUNK) >= (pc // CHUNK)
    first_head = _iota((SGU_BLOCK, LANES), 1) < HEAD_DIM
    for p in range(width // LANES):
        ls = slice(p * LANES, (p + 1) * LANES)
        wm = jnp.where(keep, ws_ref[p], jnp.zeros_like(ws_ref[p]))
        for blk in range(tm // SGU_BLOCK):
            rs = slice(blk * SGU_BLOCK, (blk + 1) * SGU_BLOCK)
            r = jnp.dot(wm, vn[rs, ls], preferred_element_type=F32)
            mixed = jnp.where(first_head, r[:SGU_BLOCK], r[SGU_BLOCK:]) + bs_ref[:, ls]
            ya_scr[rs, ls] = u[rs, ls] * mixed

    ya = ya_scr[...]
    sq = (ya * ya).astype(BF16)
    gw = gmat_ref.shape[0]
    ms = jnp.concatenate(
        [jnp.dot(sq[:, c * gw:(c + 1) * gw], gmat_ref[...], preferred_element_type=F32)
         for c in range(width // gw)], axis=1)
    ya_ref[...] = (ya * lax.rsqrt(ms + EPS) * og_ref[...]).astype(BF16)


def _fox_kernel(qt_ref, k_ref, fa_ref, vt_ref, og_ref, yb_ref,
                qaug_scr, m_scr, acc_scr, o_scr, stage_scr, *, n_heads):
    qi = pl.program_id(1)
    tq = qt_ref.shape[1]
    n_sub = tq // TQS
    blocks_per_tile = tq // TK
    sub_blocks = TQS // TK
    heads_per_group = LANES // HEAD_DIM

    rows = _iota((LANES, tq), 0)
    for h in range(n_heads):
        g, hh = divmod(h, heads_per_group)
        qt = qt_ref[g * LANES:(g + 1) * LANES, :]
        qaug_scr[h, :LANES, :] = jnp.where((rows // HEAD_DIM) == hh, qt, jnp.zeros_like(qt))
        qaug_scr[h, LANES:, :] = jnp.where(
            (rows < FORGET_PARTS * n_heads) & (lax.rem(rows, n_heads) == h), 1.0, 0.0).astype(BF16)
    m_scr[...] = jnp.full(m_scr.shape, NEG, F32)
    acc_scr[...] = jnp.zeros(acc_scr.shape, F32)

    ones_rows = jnp.ones((SUBLANES_BF16, TK), BF16)
    key_row = _iota((TK, TQS), 0)
    query_col = _iota((TK, TQS), 1)
    diag_masks = [key_row + d * TK <= query_col for d in range(sub_blocks)]

    def run(items):
        work = [(j, sub, mask, h) for (j, sub, mask) in items for h in range(n_heads)]
        loaded = {}

        def key_block(j, g):
            if (id(j), g) not in loaded:
                k0 = pl.multiple_of(j * TK, TK)
                loaded[(id(j), g)] = jnp.concatenate(
                    [k_ref[pl.ds(k0, TK), g * LANES:(g + 1) * LANES], fa_ref[pl.ds(k0, TK), :]],
                    axis=1)
            return loaded[(id(j), g)]

        def logits(w):
            j, sub, _, h = w
            return jnp.dot(key_block(j, h // heads_per_group),
                           qaug_scr[h, :, sub * TQS:(sub + 1) * TQS],
                           preferred_element_type=F32)

        def max_pass(i, s):
            _, sub, mask, h = work[i]
            if mask is not None:
                s = jnp.where(mask, s, NEG)
            slot = h * n_sub + sub
            m = m_scr[slot]
            m_new = jnp.maximum(m, jnp.max(s, axis=0, keepdims=True))
            m_scr[slot] = m_new
            stage_scr[i % STAGE_RING] = s
            return m_new, jnp.exp2(m - m_new)

        def exp_pass(i, m_new, alpha):
            j, sub, _, h = work[i]
            slot = h * n_sub + sub
            p = jnp.exp2(stage_scr[i % STAGE_RING] - m_new).astype(BF16)
            v_aug = jnp.concatenate([vt_ref[j, h * HEAD_DIM:(h + 1) * HEAD_DIM, :], ones_rows],
                                    axis=0)
            acc_scr[slot] = alpha * acc_scr[slot] + jnp.dot(v_aug, p, preferred_element_type=F32)

        n = len(work)
        pending = [logits(w) for w in work[:QK_LOOKAHEAD]]
        stats = {}
        for i in range(n + EXP_DELAY):
            if i + QK_LOOKAHEAD < n:
                pending.append(logits(work[i + QK_LOOKAHEAD]))
            if i < n:
                stats[i] = max_pass(i, pending.pop(0))
            if i >= EXP_DELAY:
                exp_pass(i - EXP_DELAY, *stats.pop(i - EXP_DELAY))

    first = qi * blocks_per_tile

    def body(t, carry):
        blocks = [t * KV_UNROLL + u for u in range(KV_UNROLL)]
        run([(j, sub, None) for j in blocks for sub in range(n_sub)])
        return carry

    lax.fori_loop(0, first // KV_UNROLL, body, 0)
    tail = []
    for b in range(blocks_per_tile):
        j = first + b
        for sub in range(n_sub):
            d = b - sub * sub_blocks
            if d < sub_blocks:
                tail.append((j, sub, diag_masks[d] if d >= 0 else None))
    run(tail)

    for h in range(n_heads):
        for sub in range(n_sub):
            acc = acc_scr[h * n_sub + sub]
            o = acc[:HEAD_DIM] / acc[HEAD_DIM:HEAD_DIM + 1]
            o_scr[h * HEAD_DIM:(h + 1) * HEAD_DIM, sub * TQS:(sub + 1) * TQS] = (
                o * lax.rsqrt(jnp.mean(o * o, axis=0, keepdims=True) + EPS))
    yb_ref[...] = (o_scr[...].T * og_ref[...]).astype(BF16)


def _back_kernel(x1_ref, ya_ref, yb_ref, wo_ref, g2_ref, w1_ref, w3_ref, w2_ref, gf_ref,
                 out_ref, g_scr, *, final_norm):
    y = jnp.concatenate([ya_ref[...], yb_ref[...]], axis=1)
    x2 = x1_ref[...] + jnp.dot(y, wo_ref[...], preferred_element_type=F32)
    h = _rmsnorm(x2, g2_ref[...]).astype(BF16)
    x3 = x2 + 0.5 * _swiglu(h, w1_ref, w3_ref, w2_ref, g_scr)
    out_ref[...] = _rmsnorm(x3, gf_ref[...]) if final_norm else x3


def _resident(shape):
    return pl.BlockSpec(shape, lambda *_: (0,) * len(shape), pipeline_mode=pl.Buffered(1))


def _row(v):
    return v.reshape(1, -1).astype(F32)


def _layer(xt, seq, final_norm, g1, w1a, w3a, w2a, gm, w_in, b_f, ln_g, ln_b, w_s, b_s, out_g,
           w_out, g2, w1b, w3b, w2b, gf):
    t_total, d = xt.shape
    d_ff = w1a.shape[1]
    n_heads = b_f.shape[0]
    width = n_heads * HEAD_DIM
    assert w_s.shape == (n_heads, SGU_BLOCK, SGU_BLOCK) and ln_g.shape == (width,)
    assert w_in.shape == (d, 5 * width + n_heads) and FORGET_PARTS * n_heads <= LANES
    assert seq % TQ == 0 and seq % TM == 0 and TM % TK == 0
    assert TQ % TQS == 0 and TQS % TK == 0 and (TQ // TK) % KV_UNROLL == 0
    assert STAGE_RING > EXP_DELAY
    assert TM % SGU_BLOCK == 0 and d_ff % FF_CHUNK == 0 and width % (2 * LANES) == 0

    bf = lambda a: a.astype(BF16)
    wu, wv = bf(w_in[:, :width]), bf(w_in[:, width:2 * width])
    wq, wk = w_in[:, 2 * width:3 * width], bf(w_in[:, 3 * width:4 * width])
    wva = w_in[:, 4 * width:5 * width]
    wf = bf(jnp.pad(w_in[:, 5 * width:], ((0, 0), (0, LANES - n_heads))))
    wqvt = bf(jnp.concatenate([wq, wva], axis=1).T)
    bf_row = jnp.pad(_row(b_f), ((0, 0), (0, LANES - n_heads)))
    ws_pairs = bf(w_s.reshape(n_heads // 2, 2 * SGU_BLOCK, SGU_BLOCK))
    bs_full = jnp.repeat(b_s.T.astype(F32), HEAD_DIM, axis=1)
    gw = 2 * LANES
    gmat = bf(jnp.where((jnp.arange(gw)[:, None] // HEAD_DIM) == (jnp.arange(gw)[None, :] // HEAD_DIM),
                        1.0 / HEAD_DIM, 0.0))
    og = _row(out_g)

    n_tiles = t_total // TM
    tile = lambda cols: pl.BlockSpec((TM, cols), lambda i: (i, 0))
    front = pl.pallas_call(
        functools.partial(_front_kernel, tiles_per_seq=seq // TM,
                          scale=HEAD_DIM ** -0.5 * LOG2E),
        name="front",
        grid=(n_tiles,),
        in_specs=[tile(d), _resident((1, d)), _resident((d, d_ff)), _resident((d, d_ff)),
                  _resident((d_ff, d)), _resident((1, d)), _resident((d, width)),
                  _resident((d, width)), _resident((d, width)), _resident((d, LANES)),
                  _resident((2 * width, d)), _resident((1, LANES)), _resident((1, width)),
                  _resident((1, width)), _resident(ws_pairs.shape), _resident((SGU_BLOCK, width)),
                  _resident((1, width)), _resident((gw, gw))],
        out_specs=[tile(d), tile(width), tile(width), tile(LANES),
                   pl.BlockSpec((1, width, TM), lambda i: (i, 0, 0)),
                   pl.BlockSpec((TM // TK, width, TK), lambda i: (i, 0, 0))],
        out_shape=[jax.ShapeDtypeStruct((t_total, d), F32),
                   jax.ShapeDtypeStruct((t_total, width), BF16),
                   jax.ShapeDtypeStruct((t_total, width), BF16),
                   jax.ShapeDtypeStruct((t_total, LANES), BF16),
                   jax.ShapeDtypeStruct((t_total // TM, width, TM), BF16),
                   jax.ShapeDtypeStruct((t_total // TK, width, TK), BF16)],
        scratch_shapes=[pltpu.VMEM((TM, d_ff), BF16), pltpu.VMEM((TM, width), F32),
                        pltpu.VMEM((1, LANES), F32)],
        compiler_params=pltpu.CompilerParams(dimension_semantics=("arbitrary",),
                                             vmem_limit_bytes=VMEM_LIMIT),
    )
    x1, ya, k, fa, qt, vt = front(
        xt, _row(g1), bf(w1a), bf(w3a), bf(w2a), _row(gm), wu, wv, wk, wf, wqvt, bf_row,
        _row(ln_g), _row(ln_b), ws_pairs, bs_full, og[:, :width], gmat)

    n_batch = t_total // seq
    nq = seq // TQ
    fox = pl.pallas_call(
        functools.partial(_fox_kernel, n_heads=n_heads),
        name="fox",
        grid=(n_batch, nq),
        in_specs=[pl.BlockSpec((None, width, TQ), lambda b, q: (b * nq + q, 0, 0)),
                  pl.BlockSpec((seq, width), lambda b, q: (b, 0)),
                  pl.BlockSpec((seq, LANES), lambda b, q: (b, 0)),
                  pl.BlockSpec((seq // TK, width, TK), lambda b, q: (b, 0, 0)),
                  pl.BlockSpec((1, width), lambda b, q: (0, 0))],
        out_specs=pl.BlockSpec((TQ, width), lambda b, q: (b * nq + q, 0)),
        out_shape=jax.ShapeDtypeStruct((t_total, width), BF16),
        scratch_shapes=[pltpu.VMEM((n_heads, 2 * LANES, TQ), BF16),
                        pltpu.VMEM((n_heads * (TQ // TQS), 1, TQS), F32),
                        pltpu.VMEM((n_heads * (TQ // TQS), HEAD_DIM + SUBLANES_BF16, TQS), F32),
                        pltpu.VMEM((width, TQ), F32),
                        pltpu.VMEM((STAGE_RING, TK, TQS), F32)],
        compiler_params=pltpu.CompilerParams(
            dimension_semantics=("arbitrary", "arbitrary"),
            vmem_limit_bytes=VMEM_LIMIT),
    )
    yb = fox(qt, k, fa, vt, og[:, width:])

    back = pl.pallas_call(
        functools.partial(_back_kernel, final_norm=final_norm),
        name="back",
        grid=(n_tiles,),
        in_specs=[tile(d), tile(width), tile(width), _resident((2 * width, d)), _resident((1, d)),
                  _resident((d, d_ff)), _resident((d, d_ff)), _resident((d_ff, d)),
                  _resident((1, d))],
        out_specs=tile(d),
        out_shape=jax.ShapeDtypeStruct((t_total, d), F32),
        scratch_shapes=[pltpu.VMEM((TM, d_ff), BF16)],
        compiler_params=pltpu.CompilerParams(dimension_semantics=("arbitrary",),
                                             vmem_limit_bytes=VMEM_LIMIT),
    )
    return back(x1, ya, yb, bf(w_out), _row(g2), bf(w1b), bf(w3b), bf(w2b), _row(gf))


def kernel(x, ffn1_norm_g, ffn1_w1, ffn1_w3, ffn1_w2, mix_norm_g, w_in, fox_f_bias, sgu_ln_g, sgu_ln_b, sgu_w_s, sgu_b_s, mix_out_g, w_out, ffn2_norm_g, ffn2_w1, ffn2_w3, ffn2_w2, final_norm_g):
    n_batch, seq, d = x.shape
    depth = ffn1_w1.shape[0]
    xt = x.reshape(n_batch * seq, d)
    for l in range(depth):
        xt = _layer(xt, seq, l == depth - 1, ffn1_norm_g[l], ffn1_w1[l], ffn1_w3[l], ffn1_w2[l],
                    mix_norm_g[l], w_in[l], fox_f_bias[l], sgu_ln_g[l], sgu_ln_b[l], sgu_w_s[l],
                    sgu_b_s[l], mix_out_g[l], w_out[l], ffn2_norm_g[l], ffn2_w1[l], ffn2_w3[l],
                    ffn2_w2[l], final_norm_g)
    return xt.reshape(n_batch, seq, d)
```

```python
import functools

import jax
import jax.numpy as jnp
from jax import lax
from jax.experimental import pallas as pl
from jax.experimental.pallas import tpu as pltpu

F32 = jnp.float32
BF16 = jnp.bfloat16

HEAD_DIM = 64
CHUNK = 64
SGU_BLOCK = 128
EPS = 1e-6
LANES = 128
FORGET_PARTS = 3

TM = 512
ROW_PARTS = 2
SUBLANES_BF16 = 16
LOG2E = 1.4426950408889634

TQ = 512
TQS = 256
TK = 128
KV_UNROLL = 4
FF_CHUNK = 256
QK_LOOKAHEAD = 6
EXP_DELAY = 4
STAGE_RING = 8
NEG = -1e30
VMEM_LIMIT = 58 * 1024 * 1024


def _iota(shape, dim):
    return lax.broadcasted_iota(jnp.int32, shape, dim)


def _rmsnorm(x, g):
    return x * lax.rsqrt(jnp.mean(x * x, axis=-1, keepdims=True) + EPS) * g


def _log_sigmoid(x):
    return jnp.minimum(x, 0.0) - jnp.log1p(jnp.exp(-jnp.abs(x)))


def _swiglu(h, w1_ref, w3_ref, w2_ref, g_scr):
    d_ff = w1_ref.shape[1]
    for c in range(d_ff // FF_CHUNK):
        sl = slice(c * FF_CHUNK, (c + 1) * FF_CHUNK)
        a = jnp.dot(h, w1_ref[:, sl], preferred_element_type=F32)
        b = jnp.dot(h, w3_ref[:, sl], preferred_element_type=F32)
        g_scr[:, sl] = (a * jax.nn.sigmoid(a) * b).astype(BF16)
    return jnp.dot(g_scr[...], w2_ref[...], preferred_element_type=F32)


def _front_kernel(x_ref, g1_ref, w1_ref, w3_ref, w2_ref, gm_ref, wu_ref, wv_ref, wk_ref,
                  wf_ref, wqvt_ref, bf_ref, lng_ref, lnb_ref, ws_ref, bs_ref, og_ref,
                  gmat_ref, x1_ref, ya_ref, k_ref, fa_ref, qt_ref, vt_ref,
                  g_scr, ya_scr, carry_ref, *, tiles_per_seq, scale):
    i = pl.program_id(0)
    tm = x_ref.shape[0] // ROW_PARTS
    width = wu_ref.shape[1]
    n_heads = width // HEAD_DIM

    @pl.when(lax.rem(i, tiles_per_seq) == 0)
    def _():
        carry_ref[...] = jnp.zeros_like(carry_ref)

    pr = jnp.bitwise_and(_iota((2 * SGU_BLOCK, SGU_BLOCK), 0), SGU_BLOCK - 1)
    pc = _iota((2 * SGU_BLOCK, SGU_BLOCK), 1)
    keep = (pr // CHUNK) >= (pc // CHUNK)
    first_head = _iota((SGU_BLOCK, LANES), 1) < HEAD_DIM
    lane = _iota((tm, LANES), 1)
    row = _iota((tm, LANES), 0)

    parts = range(ROW_PARTS)
    rows_of = [slice(part * tm, (part + 1) * tm) for part in parts]
    h2_of, proj_of, gate_of = {}, {}, {}

    for part in parts:
        rows = rows_of[part]
        x = x_ref[rows, :]
        h = _rmsnorm(x, g1_ref[...]).astype(BF16)
        x1 = x + 0.5 * _swiglu(h, w1_ref, w3_ref, w2_ref, g_scr.at[rows, :])
        x1_ref[rows, :] = x1
        h2_of[part] = _rmsnorm(x1, gm_ref[...]).astype(BF16)

    for part in parts:
        rows, h2 = rows_of[part], h2_of[part]
        zt = lax.dot_general(wqvt_ref[...], h2, (((1,), (1,)), ((), ())),
                             preferred_element_type=F32)
        qt_ref[0, :, rows] = (zt[:width] * scale).astype(BF16)
        for t in range(tm // TK):
            vt_ref[part * (tm // TK) + t] = zt[width:, t * TK:(t + 1) * TK].astype(BF16)
        k_ref[rows, :] = jnp.dot(h2, wk_ref[...], preferred_element_type=F32).astype(BF16)
        proj_of[part] = (jnp.dot(h2, wf_ref[...], preferred_element_type=F32),
                         jnp.dot(h2, wu_ref[...], preferred_element_type=F32),
                         jnp.dot(h2, wv_ref[...], preferred_element_type=F32))

    for part in parts:
        rows = rows_of[part]
        zf, zu, zv = proj_of[part]
        fsum = jnp.where(lane < n_heads, _log_sigmoid(zf + bf_ref[...]), 0.0)
        shift = 1
        while shift < tm:
            fsum = fsum + jnp.where(row >= shift, pltpu.roll(fsum, shift, axis=0), 0.0)
            shift *= 2
        fsum = fsum + carry_ref[...]
        carry_ref[...] = fsum[tm - 1:tm, :]
        neg = fsum * (-LOG2E)
        hi = neg.astype(BF16)
        rem = neg - hi.astype(F32)
        mid = rem.astype(BF16)
        lo = (rem - mid.astype(F32)).astype(BF16)
        f_parts = (hi.astype(F32) + pltpu.roll(mid.astype(F32), n_heads, axis=1)
                   + pltpu.roll(lo.astype(F32), 2 * n_heads, axis=1))
        fa_ref[rows, :] = f_parts.astype(BF16)

        u = jax.nn.gelu(zu)
        gv = jax.nn.gelu(zv)
        mu = jnp.mean(gv, axis=-1, keepdims=True)
        xc = gv - mu
        var = jnp.mean(xc * xc, axis=-1, keepdims=True)
        vn = (xc * lax.rsqrt(var + EPS) * lng_ref[...] + lnb_ref[...]).astype(BF16)
        for p in range(width // LANES):
            ls = slice(p * LANES, (p + 1) * LANES)
            wm = jnp.where(keep, ws_ref[p], jnp.zeros_like(ws_ref[p]))
            for blk in range(tm // SGU_BLOCK):
                rs = slice(blk * SGU_BLOCK, (blk + 1) * SGU_BLOCK)
                r = jnp.dot(wm, vn[rs, ls], preferred_element_type=F32)
                mixed = jnp.where(first_head, r[:SGU_BLOCK], r[SGU_BLOCK:]) + bs_ref[:, ls]
                ya_scr[part * tm + blk * SGU_BLOCK:part * tm + (blk + 1) * SGU_BLOCK, ls] = (
                    u[rs, ls] * mixed)

        ya = ya_scr[rows, :]
        sq = (ya * ya).astype(BF16)
        gw = gmat_ref.shape[0]
        ms = jnp.concatenate(
            [jnp.dot(sq[:, c * gw:(c + 1) * gw], gmat_ref[...], preferred_element_type=F32)
             for c in range(width // gw)], axis=1)
        ya_ref[rows, :] = (ya * lax.rsqrt(ms + EPS) * og_ref[...]).astype(BF16)


def _fox_kernel(qt_ref, k_ref, fa_ref, vt_ref, og_ref, yb_ref,
                qaug_scr, m_scr, acc_scr, o_scr, stage_scr, *, n_heads):
    qi = pl.program_id(1)
    tq = qt_ref.shape[1]
    n_sub = tq // TQS
    blocks_per_tile = tq // TK
    sub_blocks = TQS // TK
    heads_per_group = LANES // HEAD_DIM

    rows = _iota((LANES, tq), 0)
    for h in range(n_heads):
        g, hh = divmod(h, heads_per_group)
        qt = qt_ref[g * LANES:(g + 1) * LANES, :]
        qaug_scr[h, :LANES, :] = jnp.where((rows // HEAD_DIM) == hh, qt, jnp.zeros_like(qt))
        qaug_scr[h, LANES:, :] = jnp.where(
            (rows < FORGET_PARTS * n_heads) & (lax.rem(rows, n_heads) == h), 1.0, 0.0).astype(BF16)
    m_scr[...] = jnp.full(m_scr.shape, NEG, F32)
    acc_scr[...] = jnp.zeros(acc_scr.shape, F32)

    ones_rows = jnp.ones((SUBLANES_BF16, TK), BF16)
    key_row = _iota((TK, TQS), 0)
    query_col = _iota((TK, TQS), 1)
    diag_masks = [key_row + d * TK <= query_col for d in range(sub_blocks)]

    def run(items):
        work = [(j, sub, mask, h) for (j, sub, mask) in items for h in range(n_heads)]
        loaded = {}

        def key_block(j, g):
            if (id(j), g) not in loaded:
                k0 = pl.multiple_of(j * TK, TK)
                loaded[(id(j), g)] = jnp.concatenate(
                    [k_ref[pl.ds(k0, TK), g * LANES:(g + 1) * LANES], fa_ref[pl.ds(k0, TK), :]],
                    axis=1)
            return loaded[(id(j), g)]

        def logits(w):
            j, sub, _, h = w
            return jnp.dot(key_block(j, h // heads_per_group),
                           qaug_scr[h, :, sub * TQS:(sub + 1) * TQS],
                           preferred_element_type=F32)

        def max_pass(i, s):
            _, sub, mask, h = work[i]
            if mask is not None:
                s = jnp.where(mask, s, NEG)
            slot = h * n_sub + sub
            m = m_scr[slot]
            m_new = jnp.maximum(m, jnp.max(s, axis=0, keepdims=True))
            m_scr[slot] = m_new
            stage_scr[i % STAGE_RING] = s
            return m_new, jnp.exp2(m - m_new)

        def exp_pass(i, m_new, alpha):
            j, sub, _, h = work[i]
            slot = h * n_sub + sub
            p = jnp.exp2(stage_scr[i % STAGE_RING] - m_new).astype(BF16)
            v_aug = jnp.concatenate([vt_ref[j, h * HEAD_DIM:(h + 1) * HEAD_DIM, :], ones_rows],
                                    axis=0)
            acc_scr[slot] = alpha * acc_scr[slot] + jnp.dot(v_aug, p, preferred_element_type=F32)

        n = len(work)
        pending = [logits(w) for w in work[:QK_LOOKAHEAD]]
        stats = {}
        for i in range(n + EXP_DELAY):
            if i + QK_LOOKAHEAD < n:
                pending.append(logits(work[i + QK_LOOKAHEAD]))
            if i < n:
                stats[i] = max_pass(i, pending.pop(0))
            if i >= EXP_DELAY:
                exp_pass(i - EXP_DELAY, *stats.pop(i - EXP_DELAY))

    first = qi * blocks_per_tile

    def body(t, carry):
        blocks = [t * KV_UNROLL + u for u in range(KV_UNROLL)]
        run([(j, sub, None) for j in blocks for sub in range(n_sub)])
        return carry

    lax.fori_loop(0, first // KV_UNROLL, body, 0)
    tail = []
    for b in range(blocks_per_tile):
        j = first + b
        for sub in range(n_sub):
            d = b - sub * sub_blocks
            if d < sub_blocks:
                tail.append((j, sub, diag_masks[d] if d >= 0 else None))
    run(tail)

    for h in range(n_heads):
        for sub in range(n_sub):
            acc = acc_scr[h * n_sub + sub]
            o = acc[:HEAD_DIM] / acc[HEAD_DIM:HEAD_DIM + 1]
            o_scr[h * HEAD_DIM:(h + 1) * HEAD_DIM, sub * TQS:(sub + 1) * TQS] = (
                o * lax.rsqrt(jnp.mean(o * o, axis=0, keepdims=True) + EPS))
    yb_ref[...] = (o_scr[...].T * og_ref[...]).astype(BF16)


def _back_kernel(x1_ref, ya_ref, yb_ref, wo_ref, g2_ref, w1_ref, w3_ref, w2_ref, gf_ref,
                 out_ref, g_scr, *, final_norm):
    tm = x1_ref.shape[0] // ROW_PARTS
    rows_of = [slice(part * tm, (part + 1) * tm) for part in range(ROW_PARTS)]
    x2_of = []
    for rows in rows_of:
        y = jnp.concatenate([ya_ref[rows, :], yb_ref[rows, :]], axis=1)
        x2_of.append(x1_ref[rows, :] + jnp.dot(y, wo_ref[...], preferred_element_type=F32))
    for rows, x2 in zip(rows_of, x2_of):
        h = _rmsnorm(x2, g2_ref[...]).astype(BF16)
        x3 = x2 + 0.5 * _swiglu(h, w1_ref, w3_ref, w2_ref, g_scr.at[rows, :])
        out_ref[rows, :] = _rmsnorm(x3, gf_ref[...]) if final_norm else x3


def _resident(shape):
    return pl.BlockSpec(shape, lambda *_: (0,) * len(shape), pipeline_mode=pl.Buffered(1))


def _row(v):
    return v.reshape(1, -1).astype(F32)


def _layer(xt, seq, final_norm, g1, w1a, w3a, w2a, gm, w_in, b_f, ln_g, ln_b, w_s, b_s, out_g,
           w_out, g2, w1b, w3b, w2b, gf):
    t_total, d = xt.shape
    d_ff = w1a.shape[1]
    n_heads = b_f.shape[0]
    width = n_heads * HEAD_DIM
    assert w_s.shape == (n_heads, SGU_BLOCK, SGU_BLOCK) and ln_g.shape == (width,)
    assert w_in.shape == (d, 5 * width + n_heads) and FORGET_PARTS * n_heads <= LANES
    assert seq % TQ == 0 and seq % TM == 0 and TM % TK == 0
    assert TQ % TQS == 0 and TQS % TK == 0 and (TQ // TK) % KV_UNROLL == 0
    assert STAGE_RING > EXP_DELAY
    assert (TM // ROW_PARTS) % SGU_BLOCK == 0 and (TM // ROW_PARTS) % TK == 0
    assert TM % SGU_BLOCK == 0 and d_ff % FF_CHUNK == 0 and width % (2 * LANES) == 0

    bf = lambda a: a.astype(BF16)
    wu, wv = bf(w_in[:, :width]), bf(w_in[:, width:2 * width])
    wq, wk = w_in[:, 2 * width:3 * width], bf(w_in[:, 3 * width:4 * width])
    wva = w_in[:, 4 * width:5 * width]
    wf = bf(jnp.pad(w_in[:, 5 * width:], ((0, 0), (0, LANES - n_heads))))
    wqvt = bf(jnp.concatenate([wq, wva], axis=1).T)
    bf_row = jnp.pad(_row(b_f), ((0, 0), (0, LANES - n_heads)))
    ws_pairs = bf(w_s.reshape(n_heads // 2, 2 * SGU_BLOCK, SGU_BLOCK))
    bs_full = jnp.repeat(b_s.T.astype(F32), HEAD_DIM, axis=1)
    gw = 2 * LANES
    gmat = bf(jnp.where((jnp.arange(gw)[:, None] // HEAD_DIM) == (jnp.arange(gw)[None, :] // HEAD_DIM),
                        1.0 / HEAD_DIM, 0.0))
    og = _row(out_g)

    n_tiles = t_total // TM
    tile = lambda cols: pl.BlockSpec((TM, cols), lambda i: (i, 0))
    front = pl.pallas_call(
        functools.partial(_front_kernel, tiles_per_seq=seq // TM,
                          scale=HEAD_DIM ** -0.5 * LOG2E),
        name="front",
        grid=(n_tiles,),
        in_specs=[tile(d), _resident((1, d)), _resident((d, d_ff)), _resident((d, d_ff)),
                  _resident((d_ff, d)), _resident((1, d)), _resident((d, width)),
                  _resident((d, width)), _resident((d, width)), _resident((d, LANES)),
                  _resident((2 * width, d)), _resident((1, LANES)), _resident((1, width)),
                  _resident((1, width)), _resident(ws_pairs.shape), _resident((SGU_BLOCK, width)),
                  _resident((1, width)), _resident((gw, gw))],
        out_specs=[tile(d), tile(width), tile(width), tile(LANES),
                   pl.BlockSpec((1, width, TM), lambda i: (i, 0, 0)),
                   pl.BlockSpec((TM // TK, width, TK), lambda i: (i, 0, 0))],
        out_shape=[jax.ShapeDtypeStruct((t_total, d), F32),
                   jax.ShapeDtypeStruct((t_total, width), BF16),
                   jax.ShapeDtypeStruct((t_total, width), BF16),
                   jax.ShapeDtypeStruct((t_total, LANES), BF16),
                   jax.ShapeDtypeStruct((t_total // TM, width, TM), BF16),
                   jax.ShapeDtypeStruct((t_total // TK, width, TK), BF16)],
        scratch_shapes=[pltpu.VMEM((TM, d_ff), BF16), pltpu.VMEM((TM, width), F32),
                        pltpu.VMEM((1, LANES), F32)],
        compiler_params=pltpu.CompilerParams(dimension_semantics=("arbitrary",),
                                             vmem_limit_bytes=VMEM_LIMIT),
    )
    x1, ya, k, fa, qt, vt = front(
        xt, _row(g1), bf(w1a), bf(w3a), bf(w2a), _row(gm), wu, wv, wk, wf, wqvt, bf_row,
        _row(ln_g), _row(ln_b), ws_pairs, bs_full, og[:, :width], gmat)

    n_batch = t_total // seq
    nq = seq // TQ
    fox = pl.pallas_call(
        functools.partial(_fox_kernel, n_heads=n_heads),
        name="fox",
        grid=(n_batch, nq),
        in_specs=[pl.BlockSpec((None, width, TQ), lambda b, q: (b * nq + q, 0, 0)),
                  pl.BlockSpec((seq, width), lambda b, q: (b, 0)),
                  pl.BlockSpec((seq, LANES), lambda b, q: (b, 0)),
                  pl.BlockSpec((seq // TK, width, TK), lambda b, q: (b, 0, 0)),
                  pl.BlockSpec((1, width), lambda b, q: (0, 0))],
        out_specs=pl.BlockSpec((TQ, width), lambda b, q: (b * nq + q, 0)),
        out_shape=jax.ShapeDtypeStruct((t_total, width), BF16),
        scratch_shapes=[pltpu.VMEM((n_heads, 2 * LANES, TQ), BF16),
                        pltpu.VMEM((n_heads * (TQ // TQS), 1, TQS), F32),
                        pltpu.VMEM((n_heads * (TQ // TQS), HEAD_DIM + SUBLANES_BF16, TQS), F32),
                        pltpu.VMEM((width, TQ), F32),
                        pltpu.VMEM((STAGE_RING, TK, TQS), F32)],
        compiler_params=pltpu.CompilerParams(
            dimension_semantics=("arbitrary", "arbitrary"),
            vmem_limit_bytes=VMEM_LIMIT),
    )
    yb = fox(qt, k, fa, vt, og[:, width:])

    back = pl.pallas_call(
        functools.partial(_back_kernel, final_norm=final_norm),
        name="back",
        grid=(n_tiles,),
        in_specs=[tile(d), tile(width), tile(width), _resident((2 * width, d)), _resident((1, d)),
                  _resident((d, d_ff)), _resident((d, d_ff)), _resident((d_ff, d)),
                  _resident((1, d))],
        out_specs=tile(d),
        out_shape=jax.ShapeDtypeStruct((t_total, d), F32),
        scratch_shapes=[pltpu.VMEM((TM, d_ff), BF16)],
        compiler_params=pltpu.CompilerParams(dimension_semantics=("arbitrary",),
                                             vmem_limit_bytes=VMEM_LIMIT),
    )
    return back(x1, ya, yb, bf(w_out), _row(g2), bf(w1b), bf(w3b), bf(w2b), _row(gf))


def kernel(x, ffn1_norm_g, ffn1_w1, ffn1_w3, ffn1_w2, mix_norm_g, w_in, fox_f_bias, sgu_ln_g, sgu_ln_b, sgu_w_s, sgu_b_s, mix_out_g, w_out, ffn2_norm_g, ffn2_w1, ffn2_w3, ffn2_w2, final_norm_g):
    n_batch, seq, d = x.shape
    depth = ffn1_w1.shape[0]
    xt = x.reshape(n_batch * seq, d)
    for l in range(depth):
        xt = _layer(xt, seq, l == depth - 1, ffn1_norm_g[l], ffn1_w1[l], ffn1_w3[l], ffn1_w2[l],
                    mix_norm_g[l], w_in[l], fox_f_bias[l], sgu_ln_g[l], sgu_ln_b[l], sgu_w_s[l],
                    sgu_b_s[l], mix_out_g[l], w_out[l], ffn2_norm_g[l], ffn2_w1[l], ffn2_w3[l],
                    ffn2_w2[l], final_norm_g)
    return xt.reshape(n_batch, seq, d)
```

```python
import functools

import jax
import jax.numpy as jnp
from jax import lax
from jax.experimental import pallas as pl
from jax.experimental.pallas import tpu as pltpu

F32 = jnp.float32
BF16 = jnp.bfloat16

HEAD_DIM = 64
CHUNK = 64
SGU_BLOCK = 128
EPS = 1e-6
LANES = 128
FORGET_PARTS = 3

TM = 512
ROW_PARTS = 2
SUBLANES_BF16 = 16
LOG2E = 1.4426950408889634

TQ = 1024
TQS = 256
TK = 256
KV_UNROLL = 2
FF_CHUNK = 256
QK_LOOKAHEAD = 5
EXP_DELAY = 3
STAGE_RING = 6
NEG = -1e30
VMEM_LIMIT = 58 * 1024 * 1024


def _iota(shape, dim):
    return lax.broadcasted_iota(jnp.int32, shape, dim)


def _rmsnorm(x, g):
    return x * lax.rsqrt(jnp.mean(x * x, axis=-1, keepdims=True) + EPS) * g


def _log_sigmoid(x):
    return jnp.minimum(x, 0.0) - jnp.log1p(jnp.exp(-jnp.abs(x)))


def _swiglu(h, w1_ref, w3_ref, w2_ref, g_scr):
    d_ff = w1_ref.shape[1]
    for c in range(d_ff // FF_CHUNK):
        sl = slice(c * FF_CHUNK, (c + 1) * FF_CHUNK)
        a = jnp.dot(h, w1_ref[:, sl], preferred_element_type=F32)
        b = jnp.dot(h, w3_ref[:, sl], preferred_element_type=F32)
        g_scr[:, sl] = (a * jax.nn.sigmoid(a) * b).astype(BF16)
    return jnp.dot(g_scr[...], w2_ref[...], preferred_element_type=F32)


def _front_kernel(x_ref, g1_ref, w1_ref, w3_ref, w2_ref, gm_ref, wu_ref, wv_ref, wk_ref,
                  wf_ref, wqvt_ref, bf_ref, lng_ref, lnb_ref, ws_ref, bs_ref, og_ref,
                  gmat_ref, x1_ref, ya_ref, k_ref, fa_ref, qt_ref, vt_ref,
                  g_scr, ya_scr, carry_ref, *, tiles_per_seq, scale):
    i = pl.program_id(0)
    tm = x_ref.shape[0] // ROW_PARTS
    width = wu_ref.shape[1]
    n_heads = width // HEAD_DIM

    @pl.when(lax.rem(i, tiles_per_seq) == 0)
    def _():
        carry_ref[...] = jnp.zeros_like(carry_ref)

    pr = jnp.bitwise_and(_iota((2 * SGU_BLOCK, SGU_BLOCK), 0), SGU_BLOCK - 1)
    pc = _iota((2 * SGU_BLOCK, SGU_BLOCK), 1)
    keep = (pr // CHUNK) >= (pc // CHUNK)
    first_head = _iota((SGU_BLOCK, LANES), 1) < HEAD_DIM
    lane = _iota((tm, LANES), 1)
    row = _iota((tm, LANES), 0)

    parts = range(ROW_PARTS)
    rows_of = [slice(part * tm, (part + 1) * tm) for part in parts]
    h2_of, proj_of, gate_of = {}, {}, {}

    for part in parts:
        rows = rows_of[part]
        x = x_ref[rows, :]
        h = _rmsnorm(x, g1_ref[...]).astype(BF16)
        x1 = x + 0.5 * _swiglu(h, w1_ref, w3_ref, w2_ref, g_scr.at[rows, :])
        x1_ref[rows, :] = x1
        h2_of[part] = _rmsnorm(x1, gm_ref[...]).astype(BF16)

    for part in parts:
        rows, h2 = rows_of[part], h2_of[part]
        zt = lax.dot_general(wqvt_ref[...], h2, (((1,), (1,)), ((), ())),
                             preferred_element_type=F32)
        qt_ref[0, :, rows] = (zt[:width] * scale).astype(BF16)
        for t in range(tm // TK):
            vt_ref[part * (tm // TK) + t] = zt[width:, t * TK:(t + 1) * TK].astype(BF16)
        k_ref[rows, :] = jnp.dot(h2, wk_ref[...], preferred_element_type=F32).astype(BF16)
        proj_of[part] = (jnp.dot(h2, wf_ref[...], preferred_element_type=F32),
                         jnp.dot(h2, wu_ref[...], preferred_element_type=F32),
                         jnp.dot(h2, wv_ref[...], preferred_element_type=F32))

    for part in parts:
        rows = rows_of[part]
        zf, zu, zv = proj_of[part]
        fsum = jnp.where(lane < n_heads, _log_sigmoid(zf + bf_ref[...]), 0.0)
        shift = 1
        while shift < tm:
            fsum = fsum + jnp.where(row >= shift, pltpu.roll(fsum, shift, axis=0), 0.0)
            shift *= 2
        fsum = fsum + carry_ref[...]
        carry_ref[...] = fsum[tm - 1:tm, :]
        neg = fsum * (-LOG2E)
        hi = neg.astype(BF16)
        rem = neg - hi.astype(F32)
        mid = rem.astype(BF16)
        lo = (rem - mid.astype(F32)).astype(BF16)
        f_parts = (hi.astype(F32) + pltpu.roll(mid.astype(F32), n_heads, axis=1)
                   + pltpu.roll(lo.astype(F32), 2 * n_heads, axis=1))
        fa_ref[rows, :] = f_parts.astype(BF16)

        u = jax.nn.gelu(zu)
        gv = jax.nn.gelu(zv)
        mu = jnp.mean(gv, axis=-1, keepdims=True)
        xc = gv - mu
        var = jnp.mean(xc * xc, axis=-1, keepdims=True)
        vn = (xc * lax.rsqrt(var + EPS) * lng_ref[...] + lnb_ref[...]).astype(BF16)
        for p in range(width // LANES):
            ls = slice(p * LANES, (p + 1) * LANES)
            wm = jnp.where(keep, ws_ref[p], jnp.zeros_like(ws_ref[p]))
            for blk in range(tm // SGU_BLOCK):
                rs = slice(blk * SGU_BLOCK, (blk + 1) * SGU_BLOCK)
                r = jnp.dot(wm, vn[rs, ls], preferred_element_type=F32)
                mixed = jnp.where(first_head, r[:SGU_BLOCK], r[SGU_BLOCK:]) + bs_ref[:, ls]
                ya_scr[part * tm + blk * SGU_BLOCK:part * tm + (blk + 1) * SGU_BLOCK, ls] = (
                    u[rs, ls] * mixed)

        ya = ya_scr[rows, :]
        sq = (ya * ya).astype(BF16)
        gw = gmat_ref.shape[0]
        ms = jnp.concatenate(
            [jnp.dot(sq[:, c * gw:(c + 1) * gw], gmat_ref[...], preferred_element_type=F32)
             for c in range(width // gw)], axis=1)
        ya_ref[rows, :] = (ya * lax.rsqrt(ms + EPS) * og_ref[...]).astype(BF16)


def _fox_kernel(qt_ref, k_ref, fa_ref, vt_ref, og_ref, yb_ref,
                qaug_scr, m_scr, acc_scr, o_scr, stage_scr, *, n_heads):
    qi = pl.program_id(1)
    q_tiles, _, tm = qt_ref.shape
    tq = q_tiles * tm
    n_sub = tq // TQS
    blocks_per_tile = tq // TK
    sub_blocks = TQS // TK
    heads_per_group = LANES // HEAD_DIM

    rows = _iota((LANES, tm), 0)
    for h in range(n_heads):
        g, hh = divmod(h, heads_per_group)
        ones_part = jnp.where((rows < FORGET_PARTS * n_heads) & (lax.rem(rows, n_heads) == h),
                              1.0, 0.0).astype(BF16)
        for t in range(q_tiles):
            qt = qt_ref[t, g * LANES:(g + 1) * LANES, :]
            cols = slice(t * tm, (t + 1) * tm)
            qaug_scr[h, :LANES, cols] = jnp.where((rows // HEAD_DIM) == hh, qt, jnp.zeros_like(qt))
            qaug_scr[h, LANES:, cols] = ones_part
    m_scr[...] = jnp.full(m_scr.shape, NEG, F32)
    acc_scr[...] = jnp.zeros(acc_scr.shape, F32)

    ones_rows = jnp.ones((SUBLANES_BF16, TK), BF16)
    key_row = _iota((TK, TQS), 0)
    query_col = _iota((TK, TQS), 1)
    diag_masks = [key_row + d * TK <= query_col for d in range(sub_blocks)]

    def run(items):
        work = [(j, sub, mask, h) for (j, sub, mask) in items for h in range(n_heads)]
        loaded = {}

        def key_block(j, g):
            if (id(j), g) not in loaded:
                k0 = pl.multiple_of(j * TK, TK)
                loaded[(id(j), g)] = jnp.concatenate(
                    [k_ref[pl.ds(k0, TK), g * LANES:(g + 1) * LANES], fa_ref[pl.ds(k0, TK), :]],
                    axis=1)
            return loaded[(id(j), g)]

        def logits(w):
            j, sub, _, h = w
            return jnp.dot(key_block(j, h // heads_per_group),
                           qaug_scr[h, :, sub * TQS:(sub + 1) * TQS],
                           preferred_element_type=F32)

        def max_pass(i, s):
            _, sub, mask, h = work[i]
            if mask is not None:
                s = jnp.where(mask, s, NEG)
            slot = h * n_sub + sub
            m = m_scr[slot]
            m_new = jnp.maximum(m, jnp.max(s, axis=0, keepdims=True))
            m_scr[slot] = m_new
            stage_scr[i % STAGE_RING] = s
            return m_new, jnp.exp2(m - m_new)

        def exp_pass(i, m_new, alpha):
            j, sub, _, h = work[i]
            slot = h * n_sub + sub
            p = jnp.exp2(stage_scr[i % STAGE_RING] - m_new).astype(BF16)
            v_aug = jnp.concatenate([vt_ref[j, h * HEAD_DIM:(h + 1) * HEAD_DIM, :], ones_rows],
                                    axis=0)
            acc_scr[slot] = alpha * acc_scr[slot] + jnp.dot(v_aug, p, preferred_element_type=F32)

        n = len(work)
        pending = [logits(w) for w in work[:QK_LOOKAHEAD]]
        stats = {}
        for i in range(n + EXP_DELAY):
            if i + QK_LOOKAHEAD < n:
                pending.append(logits(work[i + QK_LOOKAHEAD]))
            if i < n:
                stats[i] = max_pass(i, pending.pop(0))
            if i >= EXP_DELAY:
                exp_pass(i - EXP_DELAY, *stats.pop(i - EXP_DELAY))

    first = qi * blocks_per_tile

    def body(t, carry):
        blocks = [t * KV_UNROLL + u for u in range(KV_UNROLL)]
        run([(j, sub, None) for j in blocks for sub in range(n_sub)])
        return carry

    lax.fori_loop(0, first // KV_UNROLL, body, 0)
    tail = []
    for b in range(blocks_per_tile):
        j = first + b
        for sub in range(n_sub):
            d = b - sub * sub_blocks
            if d < sub_blocks:
                tail.append((j, sub, diag_masks[d] if d >= 0 else None))
    run(tail)

    for h in range(n_heads):
        for sub in range(n_sub):
            acc = acc_scr[h * n_sub + sub]
            o = acc[:HEAD_DIM] / acc[HEAD_DIM:HEAD_DIM + 1]
            o_scr[h * HEAD_DIM:(h + 1) * HEAD_DIM, sub * TQS:(sub + 1) * TQS] = (
                o * lax.rsqrt(jnp.mean(o * o, axis=0, keepdims=True) + EPS))
    yb_ref[...] = (o_scr[...].T * og_ref[...]).astype(BF16)


def _back_kernel(x1_ref, ya_ref, yb_ref, wo_ref, g2_ref, w1_ref, w3_ref, w2_ref, gf_ref,
                 out_ref, g_scr, *, final_norm):
    tm = x1_ref.shape[0] // ROW_PARTS
    rows_of = [slice(part * tm, (part + 1) * tm) for part in range(ROW_PARTS)]
    x2_of = []
    for rows in rows_of:
        y = jnp.concatenate([ya_ref[rows, :], yb_ref[rows, :]], axis=1)
        x2_of.append(x1_ref[rows, :] + jnp.dot(y, wo_ref[...], preferred_element_type=F32))
    for rows, x2 in zip(rows_of, x2_of):
        h = _rmsnorm(x2, g2_ref[...]).astype(BF16)
        x3 = x2 + 0.5 * _swiglu(h, w1_ref, w3_ref, w2_ref, g_scr.at[rows, :])
        out_ref[rows, :] = _rmsnorm(x3, gf_ref[...]) if final_norm else x3


def _resident(shape):
    return pl.BlockSpec(shape, lambda *_: (0,) * len(shape), pipeline_mode=pl.Buffered(1))


def _row(v):
    return v.reshape(1, -1).astype(F32)


def _layer(xt, seq, final_norm, g1, w1a, w3a, w2a, gm, w_in, b_f, ln_g, ln_b, w_s, b_s, out_g,
           w_out, g2, w1b, w3b, w2b, gf):
    t_total, d = xt.shape
    d_ff = w1a.shape[1]
    n_heads = b_f.shape[0]
    width = n_heads * HEAD_DIM
    assert w_s.shape == (n_heads, SGU_BLOCK, SGU_BLOCK) and ln_g.shape == (width,)
    assert w_in.shape == (d, 5 * width + n_heads) and FORGET_PARTS * n_heads <= LANES
    assert seq % TQ == 0 and TQ % TM == 0 and TM % TQS == 0 and TM % TK == 0
    assert TQ % TQS == 0 and TQS % TK == 0 and (TQ // TK) % KV_UNROLL == 0
    assert STAGE_RING > EXP_DELAY
    assert (TM // ROW_PARTS) % SGU_BLOCK == 0 and (TM // ROW_PARTS) % TK == 0
    assert TM % SGU_BLOCK == 0 and d_ff % FF_CHUNK == 0 and width % (2 * LANES) == 0

    bf = lambda a: a.astype(BF16)
    wu, wv = bf(w_in[:, :width]), bf(w_in[:, width:2 * width])
    wq, wk = w_in[:, 2 * width:3 * width], bf(w_in[:, 3 * width:4 * width])
    wva = w_in[:, 4 * width:5 * width]
    wf = bf(jnp.pad(w_in[:, 5 * width:], ((0, 0), (0, LANES - n_heads))))
    wqvt = bf(jnp.concatenate([wq, wva], axis=1).T)
    bf_row = jnp.pad(_row(b_f), ((0, 0), (0, LANES - n_heads)))
    ws_pairs = bf(w_s.reshape(n_heads // 2, 2 * SGU_BLOCK, SGU_BLOCK))
    bs_full = jnp.repeat(b_s.T.astype(F32), HEAD_DIM, axis=1)
    gw = 2 * LANES
    gmat = bf(jnp.where((jnp.arange(gw)[:, None] // HEAD_DIM) == (jnp.arange(gw)[None, :] // HEAD_DIM),
                        1.0 / HEAD_DIM, 0.0))
    og = _row(out_g)

    n_tiles = t_total // TM
    tile = lambda cols: pl.BlockSpec((TM, cols), lambda i: (i, 0))
    front = pl.pallas_call(
        functools.partial(_front_kernel, tiles_per_seq=seq // TM,
                          scale=HEAD_DIM ** -0.5 * LOG2E),
        name="front",
        grid=(n_tiles,),
        in_specs=[tile(d), _resident((1, d)), _resident((d, d_ff)), _resident((d, d_ff)),
                  _resident((d_ff, d)), _resident((1, d)), _resident((d, width)),
                  _resident((d, width)), _resident((d, width)), _resident((d, LANES)),
                  _resident((2 * width, d)), _resident((1, LANES)), _resident((1, width)),
                  _resident((1, width)), _resident(ws_pairs.shape), _resident((SGU_BLOCK, width)),
                  _resident((1, width)), _resident((gw, gw))],
        out_specs=[tile(d), tile(width), tile(width), tile(LANES),
                   pl.BlockSpec((1, width, TM), lambda i: (i, 0, 0)),
                   pl.BlockSpec((TM // TK, width, TK), lambda i: (i, 0, 0))],
        out_shape=[jax.ShapeDtypeStruct((t_total, d), F32),
                   jax.ShapeDtypeStruct((t_total, width), BF16),
                   jax.ShapeDtypeStruct((t_total, width), BF16),
                   jax.ShapeDtypeStruct((t_total, LANES), BF16),
                   jax.ShapeDtypeStruct((t_total // TM, width, TM), BF16),
                   jax.ShapeDtypeStruct((t_total // TK, width, TK), BF16)],
        scratch_shapes=[pltpu.VMEM((TM, d_ff), BF16), pltpu.VMEM((TM, width), F32),
                        pltpu.VMEM((1, LANES), F32)],
        compiler_params=pltpu.CompilerParams(dimension_semantics=("arbitrary",),
                                             vmem_limit_bytes=VMEM_LIMIT),
    )
    x1, ya, k, fa, qt, vt = front(
        xt, _row(g1), bf(w1a), bf(w3a), bf(w2a), _row(gm), wu, wv, wk, wf, wqvt, bf_row,
        _row(ln_g), _row(ln_b), ws_pairs, bs_full, og[:, :width], gmat)

    n_batch = t_total // seq
    nq = seq // TQ
    fox = pl.pallas_call(
        functools.partial(_fox_kernel, n_heads=n_heads),
        name="fox",
        grid=(n_batch, nq),
        in_specs=[pl.BlockSpec((TQ // TM, width, TM), lambda b, q: (b * nq + q, 0, 0)),
                  pl.BlockSpec((seq, width), lambda b, q: (b, 0)),
                  pl.BlockSpec((seq, LANES), lambda b, q: (b, 0)),
                  pl.BlockSpec((seq // TK, width, TK), lambda b, q: (b, 0, 0)),
                  pl.BlockSpec((1, width), lambda b, q: (0, 0))],
        out_specs=pl.BlockSpec((TQ, width), lambda b, q: (b * nq + q, 0)),
        out_shape=jax.ShapeDtypeStruct((t_total, width), BF16),
        scratch_shapes=[pltpu.VMEM((n_heads, 2 * LANES, TQ), BF16),
                        pltpu.VMEM((n_heads * (TQ // TQS), 1, TQS), F32),
                        pltpu.VMEM((n_heads * (TQ // TQS), HEAD_DIM + SUBLANES_BF16, TQS), F32),
                        pltpu.VMEM((width, TQ), F32),
                        pltpu.VMEM((STAGE_RING, TK, TQS), F32)],
        compiler_params=pltpu.CompilerParams(
            dimension_semantics=("arbitrary", "arbitrary"),
            vmem_limit_bytes=VMEM_LIMIT),
    )
    yb = fox(qt, k, fa, vt, og[:, width:])

    back = pl.pallas_call(
        functools.partial(_back_kernel, final_norm=final_norm),
        name="back",
        grid=(n_tiles,),
        in_specs=[tile(d), tile(width), tile(width), _resident((2 * width, d)), _resident((1, d)),
                  _resident((d, d_ff)), _resident((d, d_ff)), _resident((d_ff, d)),
                  _resident((1, d))],
        out_specs=tile(d),
        out_shape=jax.ShapeDtypeStruct((t_total, d), F32),
        scratch_shapes=[pltpu.VMEM((TM, d_ff), BF16)],
        compiler_params=pltpu.CompilerParams(dimension_semantics=("arbitrary",),
                                             vmem_limit_bytes=VMEM_LIMIT),
    )
    return back(x1, ya, yb, bf(w_out), _row(g2), bf(w1b), bf(w3b), bf(w2b), _row(gf))


def kernel(x, ffn1_norm_g, ffn1_w1, ffn1_w3, ffn1_w2, mix_norm_g, w_in, fox_f_bias, sgu_ln_g, sgu_ln_b, sgu_w_s, sgu_b_s, mix_out_g, w_out, ffn2_norm_g, ffn2_w1, ffn2_w3, ffn2_w2, final_norm_g):
    n_batch, seq, d = x.shape
    depth = ffn1_w1.shape[0]
    xt = x.reshape(n_batch * seq, d)
    for l in range(depth):
        xt = _layer(xt, seq, l == depth - 1, ffn1_norm_g[l], ffn1_w1[l], ffn1_w3[l], ffn1_w2[l],
                    mix_norm_g[l], w_in[l], fox_f_bias[l], sgu_ln_g[l], sgu_ln_b[l], sgu_w_s[l],
                    sgu_b_s[l], mix_out_g[l], w_out[l], ffn2_norm_g[l], ffn2_w1[l], ffn2_w3[l],
                    ffn2_w2[l], final_norm_g)
    return xt.reshape(n_batch, seq, d)
```

```python
import functools

import jax
import jax.numpy as jnp
from jax import lax
from jax.experimental import pallas as pl
from jax.experimental.pallas import tpu as pltpu

F32 = jnp.float32
BF16 = jnp.bfloat16

HEAD_DIM = 64
CHUNK = 64
SGU_BLOCK = 128
EPS = 1e-6
LANES = 128
FORGET_PARTS = 3

TM = 512
ROW_PARTS = 2
SUBLANES_BF16 = 16
LOG2E = 1.4426950408889634

TQ = 1024
TQS = 256
TK = 256
KV_UNROLL = 2
FF_CHUNK = 256
QK_LOOKAHEAD = 5
EXP_DELAY = 3
STAGE_RING = 6
NEG = -1e30
VMEM_LIMIT = 58 * 1024 * 1024


def _iota(shape, dim):
    return lax.broadcasted_iota(jnp.int32, shape, dim)


def _rmsnorm(x, g):
    return x * lax.rsqrt(jnp.mean(x * x, axis=-1, keepdims=True) + EPS) * g


def _log_sigmoid(x):
    return jnp.minimum(x, 0.0) - jnp.log1p(jnp.exp(-jnp.abs(x)))


def _swiglu(h, w1_ref, w3_ref, w2_ref, g_scr, filler=None):
    d_ff = w1_ref.shape[1]
    for c in range(d_ff // FF_CHUNK):
        sl = slice(c * FF_CHUNK, (c + 1) * FF_CHUNK)
        a = jnp.dot(h, w1_ref[:, sl], preferred_element_type=F32)
        b = jnp.dot(h, w3_ref[:, sl], preferred_element_type=F32)
        g_scr[:, sl] = (a * jax.nn.sigmoid(a) * b).astype(BF16)
        if filler is not None:
            next(filler, None)
    if filler is not None:
        for _ in filler:
            pass
    return jnp.dot(g_scr[...], w2_ref[...], preferred_element_type=F32)


def _front_kernel(x_ref, g1_ref, w1_ref, w3_ref, w2_ref, gm_ref, wu_ref, wv_ref, wk_ref,
                  wf_ref, wqvt_ref, bf_ref, lng_ref, lnb_ref, ws_ref, bs_ref, og_ref,
                  gmat_ref, x1_ref, ya_ref, k_ref, fa_ref, qt_ref, vt_ref,
                  g_scr, ya_scr, carry_ref, z_scr, *, tiles_per_seq, n_tiles, scale):
    i = pl.program_id(0)
    tm = x_ref.shape[0] // ROW_PARTS
    width = wu_ref.shape[1]
    n_heads = width // HEAD_DIM

    @pl.when(i == 0)
    def _():
        z_scr[...] = jnp.zeros_like(z_scr)

    @pl.when((i == 0) | (lax.rem(i + tiles_per_seq - 1, tiles_per_seq) == 0))
    def _():
        carry_ref[...] = jnp.zeros_like(carry_ref)

    pr = jnp.bitwise_and(_iota((2 * SGU_BLOCK, SGU_BLOCK), 0), SGU_BLOCK - 1)
    pc = _iota((2 * SGU_BLOCK, SGU_BLOCK), 1)
    keep = (pr // CHUNK) >= (pc // CHUNK)
    first_head = _iota((SGU_BLOCK, LANES), 1) < HEAD_DIM
    lane = _iota((tm, LANES), 1)
    row = _iota((tm, LANES), 0)

    parts = range(ROW_PARTS)
    rows_of = [slice(part * tm, (part + 1) * tm) for part in parts]

    def ffn_stage(part, filler=None):
        rows = rows_of[part]
        x = x_ref[rows, :]
        h = _rmsnorm(x, g1_ref[...]).astype(BF16)
        x1 = x + 0.5 * _swiglu(h, w1_ref, w3_ref, w2_ref, g_scr.at[rows, :], filler)
        x1_ref[rows, :] = x1
        return _rmsnorm(x1, gm_ref[...]).astype(BF16)

    def proj_stage(part, h2):
        rows = rows_of[part]
        zt = lax.dot_general(wqvt_ref[...], h2, (((1,), (1,)), ((), ())),
                             preferred_element_type=F32)
        qt_ref[0, :, rows] = (zt[:width] * scale).astype(BF16)
        for t in range(tm // TK):
            vt_ref[part * (tm // TK) + t] = zt[width:, t * TK:(t + 1) * TK].astype(BF16)
        k_ref[rows, :] = jnp.dot(h2, wk_ref[...], preferred_element_type=F32).astype(BF16)
        z_scr[rows, :LANES] = jnp.dot(h2, wf_ref[...], preferred_element_type=F32)
        z_scr[rows, LANES:LANES + width] = jnp.dot(h2, wu_ref[...], preferred_element_type=F32)
        z_scr[rows, LANES + width:] = jnp.dot(h2, wv_ref[...], preferred_element_type=F32)

    def gate_stage(part):
        rows = rows_of[part]
        zf = z_scr[rows, :LANES]
        zu = z_scr[rows, LANES:LANES + width]
        zv = z_scr[rows, LANES + width:]
        fsum = jnp.where(lane < n_heads, _log_sigmoid(zf + bf_ref[...]), 0.0)
        shift = 1
        while shift < tm:
            fsum = fsum + jnp.where(row >= shift, pltpu.roll(fsum, shift, axis=0), 0.0)
            shift *= 2
        fsum = fsum + carry_ref[...]
        carry_ref[...] = fsum[tm - 1:tm, :]
        neg = fsum * (-LOG2E)
        hi = neg.astype(BF16)
        rem = neg - hi.astype(F32)
        mid = rem.astype(BF16)
        lo = (rem - mid.astype(F32)).astype(BF16)
        f_parts = (hi.astype(F32) + pltpu.roll(mid.astype(F32), n_heads, axis=1)
                   + pltpu.roll(lo.astype(F32), 2 * n_heads, axis=1))
        fa_ref[rows, :] = f_parts.astype(BF16)
        yield

        u = jax.nn.gelu(zu)
        yield
        gv = jax.nn.gelu(zv)
        mu = jnp.mean(gv, axis=-1, keepdims=True)
        xc = gv - mu
        var = jnp.mean(xc * xc, axis=-1, keepdims=True)
        vn = (xc * lax.rsqrt(var + EPS) * lng_ref[...] + lnb_ref[...]).astype(BF16)
        yield
        yield
        for p in range(width // LANES):
            if p == width // LANES // 2:
                yield
            ls = slice(p * LANES, (p + 1) * LANES)
            wm = jnp.where(keep, ws_ref[p], jnp.zeros_like(ws_ref[p]))
            for blk in range(tm // SGU_BLOCK):
                rs = slice(blk * SGU_BLOCK, (blk + 1) * SGU_BLOCK)
                r = jnp.dot(wm, vn[rs, ls], preferred_element_type=F32)
                mixed = jnp.where(first_head, r[:SGU_BLOCK], r[SGU_BLOCK:]) + bs_ref[:, ls]
                ya_scr[part * tm + blk * SGU_BLOCK:part * tm + (blk + 1) * SGU_BLOCK, ls] = (
                    u[rs, ls] * mixed)
        yield

        ya = ya_scr[rows, :]
        sq = (ya * ya).astype(BF16)
        gw = gmat_ref.shape[0]
        ms = jnp.concatenate(
            [jnp.dot(sq[:, c * gw:(c + 1) * gw], gmat_ref[...], preferred_element_type=F32)
             for c in range(width // gw)], axis=1)
        ya_ref[rows, :] = (ya * lax.rsqrt(ms + EPS) * og_ref[...]).astype(BF16)

    @pl.when(i < n_tiles)
    def _():
        h2_of = {}
        for part in parts:
            h2_of[part] = ffn_stage(part, gate_stage(part))
        for part in parts:
            proj_stage(part, h2_of[part])

    @pl.when(i == n_tiles)
    def _():
        for part in parts:
            for _ in gate_stage(part):
                pass


def _fox_kernel(qt_ref, k_ref, fa_ref, vt_ref, og_ref, yb_ref,
                qaug_scr, m_scr, acc_scr, o_scr, stage_scr, *, n_heads):
    qi = pl.program_id(1)
    q_tiles, _, tm = qt_ref.shape
    tq = q_tiles * tm
    n_sub = tq // TQS
    blocks_per_tile = tq // TK
    sub_blocks = TQS // TK
    heads_per_group = LANES // HEAD_DIM

    rows = _iota((LANES, tm), 0)
    for h in range(n_heads):
        g, hh = divmod(h, heads_per_group)
        ones_part = jnp.where((rows < FORGET_PARTS * n_heads) & (lax.rem(rows, n_heads) == h),
                              1.0, 0.0).astype(BF16)
        for t in range(q_tiles):
            qt = qt_ref[t, g * LANES:(g + 1) * LANES, :]
            cols = slice(t * tm, (t + 1) * tm)
            qaug_scr[h, :LANES, cols] = jnp.where((rows // HEAD_DIM) == hh, qt, jnp.zeros_like(qt))
            qaug_scr[h, LANES:, cols] = ones_part
    m_scr[...] = jnp.full(m_scr.shape, NEG, F32)
    acc_scr[...] = jnp.zeros(acc_scr.shape, F32)

    ones_rows = jnp.ones((SUBLANES_BF16, TK), BF16)
    key_row = _iota((TK, TQS), 0)
    query_col = _iota((TK, TQS), 1)
    diag_masks = [key_row + d * TK <= query_col for d in range(sub_blocks)]

    def run(items):
        work = [(j, sub, mask, h) for (j, sub, mask) in items for h in range(n_heads)]
        loaded = {}

        def key_block(j, g):
            if (id(j), g) not in loaded:
                k0 = pl.multiple_of(j * TK, TK)
                loaded[(id(j), g)] = jnp.concatenate(
                    [k_ref[pl.ds(k0, TK), g * LANES:(g + 1) * LANES], fa_ref[pl.ds(k0, TK), :]],
                    axis=1)
            return loaded[(id(j), g)]

        def logits(w):
            j, sub, _, h = w
            return jnp.dot(key_block(j, h // heads_per_group),
                           qaug_scr[h, :, sub * TQS:(sub + 1) * TQS],
                           preferred_element_type=F32)

        def max_pass(i, s):
            _, sub, mask, h = work[i]
            if mask is not None:
                s = jnp.where(mask, s, NEG)
            slot = h * n_sub + sub
            m = m_scr[slot]
            m_new = jnp.maximum(m, jnp.max(s, axis=0, keepdims=True))
            m_scr[slot] = m_new
            stage_scr[i % STAGE_RING] = s
            return m_new, jnp.exp2(m - m_new)

        def exp_pass(i, m_new, alpha):
            j, sub, _, h = work[i]
            slot = h * n_sub + sub
            p = jnp.exp2(stage_scr[i % STAGE_RING] - m_new).astype(BF16)
            v_aug = jnp.concatenate([vt_ref[j, h * HEAD_DIM:(h + 1) * HEAD_DIM, :], ones_rows],
                                    axis=0)
            acc_scr[slot] = alpha * acc_scr[slot] + jnp.dot(v_aug, p, preferred_element_type=F32)

        n = len(work)
        pending = [logits(w) for w in work[:QK_LOOKAHEAD]]
        stats = {}
        for i in range(n + EXP_DELAY):
            if i + QK_LOOKAHEAD < n:
                pending.append(logits(work[i + QK_LOOKAHEAD]))
            if i < n:
                stats[i] = max_pass(i, pending.pop(0))
            if i >= EXP_DELAY:
                exp_pass(i - EXP_DELAY, *stats.pop(i - EXP_DELAY))

    first = qi * blocks_per_tile

    def body(t, carry):
        blocks = [t * KV_UNROLL + u for u in range(KV_UNROLL)]
        run([(j, sub, None) for j in blocks for sub in range(n_sub)])
        return carry

    lax.fori_loop(0, first // KV_UNROLL, body, 0)
    tail = []
    for b in range(blocks_per_tile):
        j = first + b
        for sub in range(n_sub):
            d = b - sub * sub_blocks
            if d < sub_blocks:
                tail.append((j, sub, diag_masks[d] if d >= 0 else None))
    run(tail)

    for h in range(n_heads):
        for sub in range(n_sub):
            acc = acc_scr[h * n_sub + sub]
            o = acc[:HEAD_DIM] / acc[HEAD_DIM:HEAD_DIM + 1]
            o_scr[h * HEAD_DIM:(h + 1) * HEAD_DIM, sub * TQS:(sub + 1) * TQS] = (
                o * lax.rsqrt(jnp.mean(o * o, axis=0, keepdims=True) + EPS))
    yb_ref[...] = (o_scr[...].T * og_ref[...]).astype(BF16)


def _back_kernel(x1_ref, ya_ref, yb_ref, wo_ref, g2_ref, w1_ref, w3_ref, w2_ref, gf_ref,
                 out_ref, g_scr, *, final_norm):
    tm = x1_ref.shape[0] // ROW_PARTS
    rows_of = [slice(part * tm, (part + 1) * tm) for part in range(ROW_PARTS)]
    x2_of = []
    for rows in rows_of:
        y = jnp.concatenate([ya_ref[rows, :], yb_ref[rows, :]], axis=1)
        x2_of.append(x1_ref[rows, :] + jnp.dot(y, wo_ref[...], preferred_element_type=F32))
    for rows, x2 in zip(rows_of, x2_of):
        h = _rmsnorm(x2, g2_ref[...]).astype(BF16)
        x3 = x2 + 0.5 * _swiglu(h, w1_ref, w3_ref, w2_ref, g_scr.at[rows, :])
        out_ref[rows, :] = _rmsnorm(x3, gf_ref[...]) if final_norm else x3


def _resident(shape):
    return pl.BlockSpec(shape, lambda *_: (0,) * len(shape), pipeline_mode=pl.Buffered(1))


def _row(v):
    return v.reshape(1, -1).astype(F32)


def _layer(xt, seq, final_norm, g1, w1a, w3a, w2a, gm, w_in, b_f, ln_g, ln_b, w_s, b_s, out_g,
           w_out, g2, w1b, w3b, w2b, gf):
    t_total, d = xt.shape
    d_ff = w1a.shape[1]
    n_heads = b_f.shape[0]
    width = n_heads * HEAD_DIM
    assert w_s.shape == (n_heads, SGU_BLOCK, SGU_BLOCK) and ln_g.shape == (width,)
    assert w_in.shape == (d, 5 * width + n_heads) and FORGET_PARTS * n_heads <= LANES
    assert seq % TQ == 0 and TQ % TM == 0 and TM % TQS == 0 and TM % TK == 0
    assert TQ % TQS == 0 and TQS % TK == 0 and (TQ // TK) % KV_UNROLL == 0
    assert STAGE_RING > EXP_DELAY
    assert (TM // ROW_PARTS) % SGU_BLOCK == 0 and (TM // ROW_PARTS) % TK == 0
    assert TM % SGU_BLOCK == 0 and d_ff % FF_CHUNK == 0 and width % (2 * LANES) == 0

    bf = lambda a: a.astype(BF16)
    wu, wv = bf(w_in[:, :width]), bf(w_in[:, width:2 * width])
    wq, wk = w_in[:, 2 * width:3 * width], bf(w_in[:, 3 * width:4 * width])
    wva = w_in[:, 4 * width:5 * width]
    wf = bf(jnp.pad(w_in[:, 5 * width:], ((0, 0), (0, LANES - n_heads))))
    wqvt = bf(jnp.concatenate([wq, wva], axis=1).T)
    bf_row = jnp.pad(_row(b_f), ((0, 0), (0, LANES - n_heads)))
    ws_pairs = bf(w_s.reshape(n_heads // 2, 2 * SGU_BLOCK, SGU_BLOCK))
    bs_full = jnp.repeat(b_s.T.astype(F32), HEAD_DIM, axis=1)
    gw = 2 * LANES
    gmat = bf(jnp.where((jnp.arange(gw)[:, None] // HEAD_DIM) == (jnp.arange(gw)[None, :] // HEAD_DIM),
                        1.0 / HEAD_DIM, 0.0))
    og = _row(out_g)

    n_tiles = t_total // TM
    tile = lambda cols: pl.BlockSpec((TM, cols), lambda i: (i, 0))
    current = lambda i: jnp.minimum(i, n_tiles - 1)
    lagged = lambda i: jnp.maximum(i - 1, 0)
    cur = lambda cols: pl.BlockSpec((TM, cols), lambda i: (current(i), 0))
    lag = lambda cols: pl.BlockSpec((TM, cols), lambda i: (lagged(i), 0))
    front = pl.pallas_call(
        functools.partial(_front_kernel, tiles_per_seq=seq // TM, n_tiles=n_tiles,
                          scale=HEAD_DIM ** -0.5 * LOG2E),
        name="front",
        grid=(n_tiles + 1,),
        in_specs=[cur(d), _resident((1, d)), _resident((d, d_ff)), _resident((d, d_ff)),
                  _resident((d_ff, d)), _resident((1, d)), _resident((d, width)),
                  _resident((d, width)), _resident((d, width)), _resident((d, LANES)),
                  _resident((2 * width, d)), _resident((1, LANES)), _resident((1, width)),
                  _resident((1, width)), _resident(ws_pairs.shape), _resident((SGU_BLOCK, width)),
                  _resident((1, width)), _resident((gw, gw))],
        out_specs=[cur(d), lag(width), cur(width), lag(LANES),
                   pl.BlockSpec((1, width, TM), lambda i: (current(i), 0, 0)),
                   pl.BlockSpec((TM // TK, width, TK), lambda i: (current(i), 0, 0))],
        out_shape=[jax.ShapeDtypeStruct((t_total, d), F32),
                   jax.ShapeDtypeStruct((t_total, width), BF16),
                   jax.ShapeDtypeStruct((t_total, width), BF16),
                   jax.ShapeDtypeStruct((t_total, LANES), BF16),
                   jax.ShapeDtypeStruct((t_total // TM, width, TM), BF16),
                   jax.ShapeDtypeStruct((t_total // TK, width, TK), BF16)],
        scratch_shapes=[pltpu.VMEM((TM, d_ff), BF16), pltpu.VMEM((TM, width), F32),
                        pltpu.VMEM((1, LANES), F32), pltpu.VMEM((TM, LANES + 2 * width), F32)],
        compiler_params=pltpu.CompilerParams(dimension_semantics=("arbitrary",),
                                             vmem_limit_bytes=VMEM_LIMIT),
    )
    x1, ya, k, fa, qt, vt = front(
        xt, _row(g1), bf(w1a), bf(w3a), bf(w2a), _row(gm), wu, wv, wk, wf, wqvt, bf_row,
        _row(ln_g), _row(ln_b), ws_pairs, bs_full, og[:, :width], gmat)

    n_batch = t_total // seq
    nq = seq // TQ
    fox = pl.pallas_call(
        functools.partial(_fox_kernel, n_heads=n_heads),
        name="fox",
        grid=(n_batch, nq),
        in_specs=[pl.BlockSpec((TQ // TM, width, TM), lambda b, q: (b * nq + q, 0, 0)),
                  pl.BlockSpec((seq, width), lambda b, q: (b, 0)),
                  pl.BlockSpec((seq, LANES), lambda b, q: (b, 0)),
                  pl.BlockSpec((seq // TK, width, TK), lambda b, q: (b, 0, 0)),
                  pl.BlockSpec((1, width), lambda b, q: (0, 0))],
        out_specs=pl.BlockSpec((TQ, width), lambda b, q: (b * nq + q, 0)),
        out_shape=jax.ShapeDtypeStruct((t_total, width), BF16),
        scratch_shapes=[pltpu.VMEM((n_heads, 2 * LANES, TQ), BF16),
                        pltpu.VMEM((n_heads * (TQ // TQS), 1, TQS), F32),
                        pltpu.VMEM((n_heads * (TQ // TQS), HEAD_DIM + SUBLANES_BF16, TQS), F32),
                        pltpu.VMEM((width, TQ), F32),
                        pltpu.VMEM((STAGE_RING, TK, TQS), F32)],
        compiler_params=pltpu.CompilerParams(
            dimension_semantics=("arbitrary", "arbitrary"),
            vmem_limit_bytes=VMEM_LIMIT),
    )
    yb = fox(qt, k, fa, vt, og[:, width:])

    back = pl.pallas_call(
        functools.partial(_back_kernel, final_norm=final_norm),
        name="back",
        grid=(n_tiles,),
        in_specs=[tile(d), tile(width), tile(width), _resident((2 * width, d)), _resident((1, d)),
                  _resident((d, d_ff)), _resident((d, d_ff)), _resident((d_ff, d)),
                  _resident((1, d))],
        out_specs=tile(d),
        out_shape=jax.ShapeDtypeStruct((t_total, d), F32),
        scratch_shapes=[pltpu.VMEM((TM, d_ff), BF16)],
        compiler_params=pltpu.CompilerParams(dimension_semantics=("arbitrary",),
                                             vmem_limit_bytes=VMEM_LIMIT),
    )
    return back(x1, ya, yb, bf(w_out), _row(g2), bf(w1b), bf(w3b), bf(w2b), _row(gf))


def kernel(x, ffn1_norm_g, ffn1_w1, ffn1_w3, ffn1_w2, mix_norm_g, w_in, fox_f_bias, sgu_ln_g, sgu_ln_b, sgu_w_s, sgu_b_s, mix_out_g, w_out, ffn2_norm_g, ffn2_w1, ffn2_w3, ffn2_w2, final_norm_g):
    n_batch, seq, d = x.shape
    depth = ffn1_w1.shape[0]
    xt = x.reshape(n_batch * seq, d)
    for l in range(depth):
        xt = _layer(xt, seq, l == depth - 1, ffn1_norm_g[l], ffn1_w1[l], ffn1_w3[l], ffn1_w2[l],
                    mix_norm_g[l], w_in[l], fox_f_bias[l], sgu_ln_g[l], sgu_ln_b[l], sgu_w_s[l],
                    sgu_b_s[l], mix_out_g[l], w_out[l], ffn2_norm_g[l], ffn2_w1[l], ffn2_w3[l],
                    ffn2_w2[l], final_norm_g)
    return xt.reshape(n_batch, seq, d)
```

```python
import functools

import jax
import jax.numpy as jnp
from jax import lax
from jax.experimental import pallas as pl
from jax.experimental.pallas import tpu as pltpu

F32 = jnp.float32
BF16 = jnp.bfloat16

HEAD_DIM = 64
CHUNK = 64
SGU_BLOCK = 128
EPS = 1e-6
LANES = 128
FORGET_PARTS = 3

TM = 512
ROW_PARTS = 2
SUBLANES_BF16 = 16
LOG2E = 1.4426950408889634

TQ = 1024
TQS = 256
TK = 256
KV_UNROLL = 2
FF_CHUNK = 256
CAST_CHUNKS = 16
QK_LOOKAHEAD = 5
EXP_DELAY = 3
STAGE_RING = 6
NEG = -1e30
VMEM_LIMIT = 58 * 1024 * 1024


def _iota(shape, dim):
    return lax.broadcasted_iota(jnp.int32, shape, dim)


def _rmsnorm(x, g):
    return x * lax.rsqrt(jnp.mean(x * x, axis=-1, keepdims=True) + EPS) * g


def _log_sigmoid(x):
    return jnp.minimum(x, 0.0) - jnp.log1p(jnp.exp(-jnp.abs(x)))


def _load_bf16(w_hbm, w_bf, stage, sem):
    rows = stage.shape[1]
    n_chunks = w_hbm.shape[0] // rows

    def copy(c):
        return pltpu.make_async_copy(w_hbm.at[pl.ds(c * rows, rows), :], stage.at[c % 2],
                                     sem.at[c % 2])

    copy(0).start()
    for c in range(n_chunks):
        if c + 1 < n_chunks:
            copy(c + 1).start()
        copy(c).wait()
        w_bf[pl.ds(c * rows, rows), :] = stage[c % 2].astype(BF16)


def _load_ffn_weights(w1_hbm, w3_hbm, w2_hbm, ffn_scr):
    w1_bf, w3_bf, w2_bf, stage_up, stage_down, sem = ffn_scr
    _load_bf16(w1_hbm, w1_bf, stage_up, sem)
    _load_bf16(w3_hbm, w3_bf, stage_up, sem)
    _load_bf16(w2_hbm, w2_bf, stage_down, sem)


def _ffn_scratch(d, d_ff):
    assert d % CAST_CHUNKS == 0 and d_ff % CAST_CHUNKS == 0
    assert (d // CAST_CHUNKS) % SUBLANES_BF16 == 0 and (d_ff // CAST_CHUNKS) % SUBLANES_BF16 == 0
    return [pltpu.VMEM((d, d_ff), BF16), pltpu.VMEM((d, d_ff), BF16), pltpu.VMEM((d_ff, d), BF16),
            pltpu.VMEM((2, d // CAST_CHUNKS, d_ff), F32),
            pltpu.VMEM((2, d_ff // CAST_CHUNKS, d), F32), pltpu.SemaphoreType.DMA((2,))]


def _swiglu(h, w1_ref, w3_ref, w2_ref, g_scr, filler=None):
    d_ff = w1_ref.shape[1]
    for c in range(d_ff // FF_CHUNK):
        sl = slice(c * FF_CHUNK, (c + 1) * FF_CHUNK)
        a = jnp.dot(h, w1_ref[:, sl], preferred_element_type=F32)
        b = jnp.dot(h, w3_ref[:, sl], preferred_element_type=F32)
        g_scr[:, sl] = (a * jax.nn.sigmoid(a) * b).astype(BF16)
        if filler is not None:
            next(filler, None)
    if filler is not None:
        for _ in filler:
            pass
    return jnp.dot(g_scr[...], w2_ref[...], preferred_element_type=F32)


def _front_kernel(x_ref, g1_ref, w1_ref, w3_ref, w2_ref, gm_ref, wu_ref, wv_ref, wk_ref,
                  wf_ref, wqvt_ref, bf_ref, lng_ref, lnb_ref, ws_ref, bs_ref, og_ref,
                  gmat_ref, x1_ref, ya_ref, k_ref, fa_ref, qt_ref, vt_ref,
                  g_scr, ya_scr, carry_ref, z_scr, *ffn_scr, tiles_per_seq, n_tiles, scale):
    i = pl.program_id(0)
    tm = x_ref.shape[0] // ROW_PARTS
    width = wu_ref.shape[1]
    n_heads = width // HEAD_DIM

    w1_bf, w3_bf, w2_bf = ffn_scr[:3]

    @pl.when(i == 0)
    def _():
        _load_ffn_weights(w1_ref, w3_ref, w2_ref, ffn_scr)
        z_scr[...] = jnp.zeros_like(z_scr)

    @pl.when((i == 0) | (lax.rem(i + tiles_per_seq - 1, tiles_per_seq) == 0))
    def _():
        carry_ref[...] = jnp.zeros_like(carry_ref)

    pr = jnp.bitwise_and(_iota((2 * SGU_BLOCK, SGU_BLOCK), 0), SGU_BLOCK - 1)
    pc = _iota((2 * SGU_BLOCK, SGU_BLOCK), 1)
    keep = (pr // CHUNK) >= (pc // CHUNK)
    first_head = _iota((SGU_BLOCK, LANES), 1) < HEAD_DIM
    lane = _iota((tm, LANES), 1)
    row = _iota((tm, LANES), 0)

    parts = range(ROW_PARTS)
    rows_of = [slice(part * tm, (part + 1) * tm) for part in parts]

    def ffn_stage(part, filler=None):
        rows = rows_of[part]
        x = x_ref[rows, :]
        h = _rmsnorm(x, g1_ref[...]).astype(BF16)
        x1 = x + 0.5 * _swiglu(h, w1_bf, w3_bf, w2_bf, g_scr.at[rows, :], filler)
        x1_ref[rows, :] = x1
        return _rmsnorm(x1, gm_ref[...]).astype(BF16)

    def proj_stage(part, h2):
        rows = rows_of[part]
        zt = lax.dot_general(wqvt_ref[...], h2, (((1,), (1,)), ((), ())),
                             preferred_element_type=F32)
        qt_ref[0, :, rows] = (zt[:width] * scale).astype(BF16)
        for t in range(tm // TK):
            vt_ref[part * (tm // TK) + t] = zt[width:, t * TK:(t + 1) * TK].astype(BF16)
        k_ref[rows, :] = jnp.dot(h2, wk_ref[...], preferred_element_type=F32).astype(BF16)
        z_scr[rows, :LANES] = jnp.dot(h2, wf_ref[...], preferred_element_type=F32)
        z_scr[rows, LANES:LANES + width] = jnp.dot(h2, wu_ref[...], preferred_element_type=F32)
        z_scr[rows, LANES + width:] = jnp.dot(h2, wv_ref[...], preferred_element_type=F32)

    def gate_stage(part):
        rows = rows_of[part]
        zf = z_scr[rows, :LANES]
        zu = z_scr[rows, LANES:LANES + width]
        zv = z_scr[rows, LANES + width:]
        fsum = jnp.where(lane < n_heads, _log_sigmoid(zf + bf_ref[...]), 0.0)
        shift = 1
        while shift < tm:
            fsum = fsum + jnp.where(row >= shift, pltpu.roll(fsum, shift, axis=0), 0.0)
            shift *= 2
        fsum = fsum + carry_ref[...]
        carry_ref[...] = fsum[tm - 1:tm, :]
        neg = fsum * (-LOG2E)
        hi = neg.astype(BF16)
        rem = neg - hi.astype(F32)
        mid = rem.astype(BF16)
        lo = (rem - mid.astype(F32)).astype(BF16)
        f_parts = (hi.astype(F32) + pltpu.roll(mid.astype(F32), n_heads, axis=1)
                   + pltpu.roll(lo.astype(F32), 2 * n_heads, axis=1))
        fa_ref[rows, :] = f_parts.astype(BF16)
        yield

        u = jax.nn.gelu(zu)
        yield
        gv = jax.nn.gelu(zv)
        mu = jnp.mean(gv, axis=-1, keepdims=True)
        xc = gv - mu
        var = jnp.mean(xc * xc, axis=-1, keepdims=True)
        vn = (xc * lax.rsqrt(var + EPS) * lng_ref[...] + lnb_ref[...]).astype(BF16)
        yield
        yield
        for p in range(width // LANES):
            if p == width // LANES // 2:
                yield
            ls = slice(p * LANES, (p + 1) * LANES)
            wm = jnp.where(keep, ws_ref[p], jnp.zeros_like(ws_ref[p]))
            for blk in range(tm // SGU_BLOCK):
                rs = slice(blk * SGU_BLOCK, (blk + 1) * SGU_BLOCK)
                r = jnp.dot(wm, vn[rs, ls], preferred_element_type=F32)
                mixed = jnp.where(first_head, r[:SGU_BLOCK], r[SGU_BLOCK:]) + bs_ref[:, ls]
                ya_scr[part * tm + blk * SGU_BLOCK:part * tm + (blk + 1) * SGU_BLOCK, ls] = (
                    u[rs, ls] * mixed)
        yield

        ya = ya_scr[rows, :]
        sq = (ya * ya).astype(BF16)
        gw = gmat_ref.shape[0]
        ms = jnp.concatenate(
            [jnp.dot(sq[:, c * gw:(c + 1) * gw], gmat_ref[...], preferred_element_type=F32)
             for c in range(width // gw)], axis=1)
        ya_ref[rows, :] = (ya * lax.rsqrt(ms + EPS) * og_ref[...]).astype(BF16)

    @pl.when(i < n_tiles)
    def _():
        h2_of = {}
        for part in parts:
            h2_of[part] = ffn_stage(part, gate_stage(part))
        for part in parts:
            proj_stage(part, h2_of[part])

    @pl.when(i == n_tiles)
    def _():
        for part in parts:
            for _ in gate_stage(part):
                pass


def _fox_kernel(qt_ref, k_ref, fa_ref, vt_ref, og_ref, yb_ref,
                qaug_scr, m_scr, acc_scr, o_scr, stage_scr, *, n_heads):
    qi = pl.program_id(1)
    q_tiles, _, tm = qt_ref.shape
    tq = q_tiles * tm
    n_sub = tq // TQS
    blocks_per_tile = tq // TK
    sub_blocks = TQS // TK
    heads_per_group = LANES // HEAD_DIM

    rows = _iota((LANES, tm), 0)
    for h in range(n_heads):
        g, hh = divmod(h, heads_per_group)
        ones_part = jnp.where((rows < FORGET_PARTS * n_heads) & (lax.rem(rows, n_heads) == h),
                              1.0, 0.0).astype(BF16)
        for t in range(q_tiles):
            qt = qt_ref[t, g * LANES:(g + 1) * LANES, :]
            cols = slice(t * tm, (t + 1) * tm)
            qaug_scr[h, :LANES, cols] = jnp.where((rows // HEAD_DIM) == hh, qt, jnp.zeros_like(qt))
            qaug_scr[h, LANES:, cols] = ones_part
    m_scr[...] = jnp.full(m_scr.shape, NEG, F32)
    acc_scr[...] = jnp.zeros(acc_scr.shape, F32)

    ones_rows = jnp.ones((SUBLANES_BF16, TK), BF16)
    key_row = _iota((TK, TQS), 0)
    query_col = _iota((TK, TQS), 1)
    diag_masks = [key_row + d * TK <= query_col for d in range(sub_blocks)]

    def run(items):
        work = [(j, sub, mask, h) for (j, sub, mask) in items for h in range(n_heads)]
        loaded = {}

        def key_block(j, g):
            if (id(j), g) not in loaded:
                k0 = pl.multiple_of(j * TK, TK)
                loaded[(id(j), g)] = jnp.concatenate(
                    [k_ref[pl.ds(k0, TK), g * LANES:(g + 1) * LANES], fa_ref[pl.ds(k0, TK), :]],
                    axis=1)
            return loaded[(id(j), g)]

        def logits(w):
            j, sub, _, h = w
            return jnp.dot(key_block(j, h // heads_per_group),
                           qaug_scr[h, :, sub * TQS:(sub + 1) * TQS],
                           preferred_element_type=F32)

        def max_pass(i, s):
            _, sub, mask, h = work[i]
            if mask is not None:
                s = jnp.where(mask, s, NEG)
            slot = h * n_sub + sub
            m = m_scr[slot]
            m_new = jnp.maximum(m, jnp.max(s, axis=0, keepdims=True))
            m_scr[slot] = m_new
            stage_scr[i % STAGE_RING] = s
            return m_new, jnp.exp2(m - m_new)

        def exp_pass(i, m_new, alpha):
            j, sub, _, h = work[i]
            slot = h * n_sub + sub
            p = jnp.exp2(stage_scr[i % STAGE_RING] - m_new).astype(BF16)
            v_aug = jnp.concatenate([vt_ref[j, h * HEAD_DIM:(h + 1) * HEAD_DIM, :], ones_rows],
                                    axis=0)
            acc_scr[slot] = alpha * acc_scr[slot] + jnp.dot(v_aug, p, preferred_element_type=F32)

        n = len(work)
        pending = [logits(w) for w in work[:QK_LOOKAHEAD]]
        stats = {}
        for i in range(n + EXP_DELAY):
            if i + QK_LOOKAHEAD < n:
                pending.append(logits(work[i + QK_LOOKAHEAD]))
            if i < n:
                stats[i] = max_pass(i, pending.pop(0))
            if i >= EXP_DELAY:
                exp_pass(i - EXP_DELAY, *stats.pop(i - EXP_DELAY))

    first = qi * blocks_per_tile

    def body(t, carry):
        blocks = [t * KV_UNROLL + u for u in range(KV_UNROLL)]
        run([(j, sub, None) for j in blocks for sub in range(n_sub)])
        return carry

    lax.fori_loop(0, first // KV_UNROLL, body, 0)
    tail = []
    for b in range(blocks_per_tile):
        j = first + b
        for sub in range(n_sub):
            d = b - sub * sub_blocks
            if d < sub_blocks:
                tail.append((j, sub, diag_masks[d] if d >= 0 else None))
    run(tail)

    for h in range(n_heads):
        for sub in range(n_sub):
            acc = acc_scr[h * n_sub + sub]
            o = acc[:HEAD_DIM] / acc[HEAD_DIM:HEAD_DIM + 1]
            o_scr[h * HEAD_DIM:(h + 1) * HEAD_DIM, sub * TQS:(sub + 1) * TQS] = (
                o * lax.rsqrt(jnp.mean(o * o, axis=0, keepdims=True) + EPS))
    yb_ref[...] = (o_scr[...].T * og_ref[...]).astype(BF16)


def _back_kernel(x1_ref, ya_ref, yb_ref, wo_ref, g2_ref, w1_ref, w3_ref, w2_ref, gf_ref,
                 out_ref, g_scr, *ffn_scr, final_norm):
    w1_bf, w3_bf, w2_bf = ffn_scr[:3]

    @pl.when(pl.program_id(0) == 0)
    def _():
        _load_ffn_weights(w1_ref, w3_ref, w2_ref, ffn_scr)

    tm = x1_ref.shape[0] // ROW_PARTS
    rows_of = [slice(part * tm, (part + 1) * tm) for part in range(ROW_PARTS)]
    x2_of = []
    for rows in rows_of:
        y = jnp.concatenate([ya_ref[rows, :], yb_ref[rows, :]], axis=1)
        x2_of.append(x1_ref[rows, :] + jnp.dot(y, wo_ref[...], preferred_element_type=F32))
    for rows, x2 in zip(rows_of, x2_of):
        h = _rmsnorm(x2, g2_ref[...]).astype(BF16)
        x3 = x2 + 0.5 * _swiglu(h, w1_bf, w3_bf, w2_bf, g_scr.at[rows, :])
        out_ref[rows, :] = _rmsnorm(x3, gf_ref[...]) if final_norm else x3


def _resident(shape):
    return pl.BlockSpec(shape, lambda *_: (0,) * len(shape), pipeline_mode=pl.Buffered(1))


def _row(v):
    return v.reshape(1, -1).astype(F32)


def _layer(xt, seq, final_norm, g1, w1a, w3a, w2a, gm, w_in, b_f, ln_g, ln_b, w_s, b_s, out_g,
           w_out, g2, w1b, w3b, w2b, gf):
    t_total, d = xt.shape
    d_ff = w1a.shape[1]
    n_heads = b_f.shape[0]
    width = n_heads * HEAD_DIM
    assert w_s.shape == (n_heads, SGU_BLOCK, SGU_BLOCK) and ln_g.shape == (width,)
    assert w_in.shape == (d, 5 * width + n_heads) and FORGET_PARTS * n_heads <= LANES
    assert seq % TQ == 0 and TQ % TM == 0 and TM % TQS == 0 and TM % TK == 0
    assert TQ % TQS == 0 and TQS % TK == 0 and (TQ // TK) % KV_UNROLL == 0
    assert STAGE_RING > EXP_DELAY
    assert (TM // ROW_PARTS) % SGU_BLOCK == 0 and (TM // ROW_PARTS) % TK == 0
    assert TM % SGU_BLOCK == 0 and d_ff % FF_CHUNK == 0 and width % (2 * LANES) == 0

    bf = lambda a: a.astype(BF16)
    wu, wv = bf(w_in[:, :width]), bf(w_in[:, width:2 * width])
    wq, wk = w_in[:, 2 * width:3 * width], bf(w_in[:, 3 * width:4 * width])
    wva = w_in[:, 4 * width:5 * width]
    wf = bf(jnp.pad(w_in[:, 5 * width:], ((0, 0), (0, LANES - n_heads))))
    wqvt = bf(jnp.concatenate([wq, wva], axis=1).T)
    bf_row = jnp.pad(_row(b_f), ((0, 0), (0, LANES - n_heads)))
    ws_pairs = bf(w_s.reshape(n_heads // 2, 2 * SGU_BLOCK, SGU_BLOCK))
    bs_full = jnp.repeat(b_s.T.astype(F32), HEAD_DIM, axis=1)
    gw = 2 * LANES
    gmat = bf(jnp.where((jnp.arange(gw)[:, None] // HEAD_DIM) == (jnp.arange(gw)[None, :] // HEAD_DIM),
                        1.0 / HEAD_DIM, 0.0))
    og = _row(out_g)

    n_tiles = t_total // TM
    hbm = pl.BlockSpec(memory_space=pl.ANY)
    tile = lambda cols: pl.BlockSpec((TM, cols), lambda i: (i, 0))
    current = lambda i: jnp.minimum(i, n_tiles - 1)
    lagged = lambda i: jnp.maximum(i - 1, 0)
    cur = lambda cols: pl.BlockSpec((TM, cols), lambda i: (current(i), 0))
    lag = lambda cols: pl.BlockSpec((TM, cols), lambda i: (lagged(i), 0))
    front = pl.pallas_call(
        functools.partial(_front_kernel, tiles_per_seq=seq // TM, n_tiles=n_tiles,
                          scale=HEAD_DIM ** -0.5 * LOG2E),
        name="front",
        grid=(n_tiles + 1,),
        in_specs=[cur(d), _resident((1, d)), hbm, hbm, hbm, _resident((1, d)), _resident((d, width)),
                  _resident((d, width)), _resident((d, width)), _resident((d, LANES)),
                  _resident((2 * width, d)), _resident((1, LANES)), _resident((1, width)),
                  _resident((1, width)), _resident(ws_pairs.shape), _resident((SGU_BLOCK, width)),
                  _resident((1, width)), _resident((gw, gw))],
        out_specs=[cur(d), lag(width), cur(width), lag(LANES),
                   pl.BlockSpec((1, width, TM), lambda i: (current(i), 0, 0)),
                   pl.BlockSpec((TM // TK, width, TK), lambda i: (current(i), 0, 0))],
        out_shape=[jax.ShapeDtypeStruct((t_total, d), F32),
                   jax.ShapeDtypeStruct((t_total, width), BF16),
                   jax.ShapeDtypeStruct((t_total, width), BF16),
                   jax.ShapeDtypeStruct((t_total, LANES), BF16),
                   jax.ShapeDtypeStruct((t_total // TM, width, TM), BF16),
                   jax.ShapeDtypeStruct((t_total // TK, width, TK), BF16)],
        scratch_shapes=[pltpu.VMEM((TM, d_ff), BF16), pltpu.VMEM((TM, width), F32),
                        pltpu.VMEM((1, LANES), F32), pltpu.VMEM((TM, LANES + 2 * width), F32)]
        + _ffn_scratch(d, d_ff),
        compiler_params=pltpu.CompilerParams(dimension_semantics=("arbitrary",),
                                             vmem_limit_bytes=VMEM_LIMIT),
    )
    x1, ya, k, fa, qt, vt = front(
        xt, _row(g1), w1a, w3a, w2a, _row(gm), wu, wv, wk, wf, wqvt, bf_row,
        _row(ln_g), _row(ln_b), ws_pairs, bs_full, og[:, :width], gmat)

    n_batch = t_total // seq
    nq = seq // TQ
    fox = pl.pallas_call(
        functools.partial(_fox_kernel, n_heads=n_heads),
        name="fox",
        grid=(n_batch, nq),
        in_specs=[pl.BlockSpec((TQ // TM, width, TM), lambda b, q: (b * nq + q, 0, 0)),
                  pl.BlockSpec((seq, width), lambda b, q: (b, 0)),
                  pl.BlockSpec((seq, LANES), lambda b, q: (b, 0)),
                  pl.BlockSpec((seq // TK, width, TK), lambda b, q: (b, 0, 0)),
                  pl.BlockSpec((1, width), lambda b, q: (0, 0))],
        out_specs=pl.BlockSpec((TQ, width), lambda b, q: (b * nq + q, 0)),
        out_shape=jax.ShapeDtypeStruct((t_total, width), BF16),
        scratch_shapes=[pltpu.VMEM((n_heads, 2 * LANES, TQ), BF16),
                        pltpu.VMEM((n_heads * (TQ // TQS), 1, TQS), F32),
                        pltpu.VMEM((n_heads * (TQ // TQS), HEAD_DIM + SUBLANES_BF16, TQS), F32),
                        pltpu.VMEM((width, TQ), F32),
                        pltpu.VMEM((STAGE_RING, TK, TQS), F32)],
        compiler_params=pltpu.CompilerParams(
            dimension_semantics=("arbitrary", "arbitrary"),
            vmem_limit_bytes=VMEM_LIMIT),
    )
    yb = fox(qt, k, fa, vt, og[:, width:])

    back = pl.pallas_call(
        functools.partial(_back_kernel, final_norm=final_norm),
        name="back",
        grid=(n_tiles,),
        in_specs=[tile(d), tile(width), tile(width), _resident((2 * width, d)), _resident((1, d)),
                  hbm, hbm, hbm, _resident((1, d))],
        out_specs=tile(d),
        out_shape=jax.ShapeDtypeStruct((t_total, d), F32),
        scratch_shapes=[pltpu.VMEM((TM, d_ff), BF16)] + _ffn_scratch(d, d_ff),
        compiler_params=pltpu.CompilerParams(dimension_semantics=("arbitrary",),
                                             vmem_limit_bytes=VMEM_LIMIT),
    )
    return back(x1, ya, yb, bf(w_out), _row(g2), w1b, w3b, w2b, _row(gf))


def kernel(x, ffn1_norm_g, ffn1_w1, ffn1_w3, ffn1_w2, mix_norm_g, w_in, fox_f_bias, sgu_ln_g, sgu_ln_b, sgu_w_s, sgu_b_s, mix_out_g, w_out, ffn2_norm_g, ffn2_w1, ffn2_w3, ffn2_w2, final_norm_g):
    n_batch, seq, d = x.shape
    depth = ffn1_w1.shape[0]
    xt = x.reshape(n_batch * seq, d)
    for l in range(depth):
        xt = _layer(xt, seq, l == depth - 1, ffn1_norm_g[l], ffn1_w1[l], ffn1_w3[l], ffn1_w2[l],
                    mix_norm_g[l], w_in[l], fox_f_bias[l], sgu_ln_g[l], sgu_ln_b[l], sgu_w_s[l],
                    sgu_b_s[l], mix_out_g[l], w_out[l], ffn2_norm_g[l], ffn2_w1[l], ffn2_w3[l],
                    ffn2_w2[l], final_norm_g)
    return xt.reshape(n_batch, seq, d)
```

```python
import functools

import jax
import jax.numpy as jnp
from jax import lax
from jax.experimental import pallas as pl
from jax.experimental.pallas import tpu as pltpu

F32 = jnp.float32
BF16 = jnp.bfloat16

HEAD_DIM = 64
CHUNK = 64
SGU_BLOCK = 128
EPS = 1e-6
LANES = 128
FORGET_PARTS = 3

TM = 512
ROW_PARTS = 2
SUBLANES_BF16 = 16
LOG2E = 1.4426950408889634

TQ = 1024
TQS = 256
TK = 256
KV_UNROLL = 4
FF_CHUNK = 256
QK_LOOKAHEAD = 5
EXP_DELAY = 3
STAGE_RING = 6
NEG = -1e30
VMEM_LIMIT = 58 * 1024 * 1024


def _iota(shape, dim):
    return lax.broadcasted_iota(jnp.int32, shape, dim)


def _rmsnorm(x, g):
    return x * lax.rsqrt(jnp.mean(x * x, axis=-1, keepdims=True) + EPS) * g


def _log_sigmoid(x):
    return jnp.minimum(x, 0.0) - jnp.log1p(jnp.exp(-jnp.abs(x)))


def _swiglu(h, w1_ref, w3_ref, w2_ref, g_scr, filler=None):
    d_ff = w1_ref.shape[1]
    for c in range(d_ff // FF_CHUNK):
        sl = slice(c * FF_CHUNK, (c + 1) * FF_CHUNK)
        a = jnp.dot(h, w1_ref[:, sl], preferred_element_type=F32)
        b = jnp.dot(h, w3_ref[:, sl], preferred_element_type=F32)
        g_scr[:, sl] = (a * jax.nn.sigmoid(a) * b).astype(BF16)
        if filler is not None:
            next(filler, None)
    if filler is not None:
        for _ in filler:
            pass
    return jnp.dot(g_scr[...], w2_ref[...], preferred_element_type=F32)


def _front_kernel(x_ref, g1_ref, w1_ref, w3_ref, w2_ref, gm_ref, wu_ref, wv_ref, wk_ref,
                  wqvft_ref, bf_ref, lng_ref, lnb_ref, ws_ref, bs_ref, og_ref,
                  gmat_ref, x1_ref, ya_ref, k_ref, fa_ref, qt_ref, vt_ref,
                  g_scr, ya_scr, carry_ref, z_scr, zft_scr, *, tiles_per_seq, n_tiles, scale):
    i = pl.program_id(0)
    tm = x_ref.shape[0] // ROW_PARTS
    width = wu_ref.shape[1]
    n_heads = width // HEAD_DIM

    @pl.when(i == 0)
    def _():
        z_scr[...] = jnp.zeros_like(z_scr)
        zft_scr[...] = jnp.zeros_like(zft_scr)

    @pl.when((i == 0) | (lax.rem(i + tiles_per_seq - 1, tiles_per_seq) == 0))
    def _():
        carry_ref[...] = jnp.zeros_like(carry_ref)

    pr = jnp.bitwise_and(_iota((2 * SGU_BLOCK, SGU_BLOCK), 0), SGU_BLOCK - 1)
    pc = _iota((2 * SGU_BLOCK, SGU_BLOCK), 1)
    keep = (pr // CHUNK) >= (pc // CHUNK)
    first_head = _iota((SGU_BLOCK, LANES), 1) < HEAD_DIM
    f_rows = zft_scr.shape[0]
    f_head = _iota((f_rows, tm), 0)
    f_token = _iota((f_rows, tm), 1)

    parts = range(ROW_PARTS)
    rows_of = [slice(part * tm, (part + 1) * tm) for part in parts]

    def ffn_stage(part, filler=None):
        rows = rows_of[part]
        x = x_ref[rows, :]
        h = _rmsnorm(x, g1_ref[...]).astype(BF16)
        x1 = x + 0.5 * _swiglu(h, w1_ref, w3_ref, w2_ref, g_scr.at[rows, :], filler)
        x1_ref[rows, :] = x1
        return _rmsnorm(x1, gm_ref[...]).astype(BF16)

    def proj_stage(part, h2):
        rows = rows_of[part]
        zt = lax.dot_general(wqvft_ref[...], h2, (((1,), (1,)), ((), ())),
                             preferred_element_type=F32)
        qt_ref[0, :, rows] = (zt[:width] * scale).astype(BF16)
        for t in range(tm // TK):
            vt_ref[part * (tm // TK) + t] = zt[width:2 * width, t * TK:(t + 1) * TK].astype(BF16)
        zft_scr[:, rows] = zt[2 * width:]
        k_ref[rows, :] = jnp.dot(h2, wk_ref[...], preferred_element_type=F32).astype(BF16)
        z_scr[rows, :width] = jnp.dot(h2, wu_ref[...], preferred_element_type=F32)
        z_scr[rows, width:] = jnp.dot(h2, wv_ref[...], preferred_element_type=F32)

    def gate_stage(part):
        rows = rows_of[part]
        zu = z_scr[rows, :width]
        zv = z_scr[rows, width:]
        fsum = jnp.where(f_head < n_heads, _log_sigmoid(zft_scr[:, rows] + bf_ref[...]), 0.0)
        shift = 1
        while shift < tm:
            fsum = fsum + jnp.where(f_token >= shift, pltpu.roll(fsum, shift, axis=1), 0.0)
            shift *= 2
        fsum = fsum + carry_ref[...]
        carry_ref[...] = fsum[:, tm - 1:tm]
        neg = fsum * (-LOG2E)
        hi = neg.astype(BF16)
        rem = neg - hi.astype(F32)
        mid = rem.astype(BF16)
        lo = (rem - mid.astype(F32)).astype(BF16)
        f_parts = jnp.concatenate(
            [part_t.astype(F32)[:n_heads] for part_t in (hi, mid, lo)]
            + [jnp.zeros((LANES - FORGET_PARTS * n_heads, tm), F32)], axis=0)
        fa_ref[rows, :] = f_parts.T.astype(BF16)
        yield

        u = jax.nn.gelu(zu)
        yield
        gv = jax.nn.gelu(zv)
        mu = jnp.mean(gv, axis=-1, keepdims=True)
        xc = gv - mu
        var = jnp.mean(xc * xc, axis=-1, keepdims=True)
        vn = (xc * lax.rsqrt(var + EPS) * lng_ref[...] + lnb_ref[...]).astype(BF16)
        yield
        yield
        for p in range(width // LANES):
            if p == width // LANES // 2:
                yield
            ls = slice(p * LANES, (p + 1) * LANES)
            wm = jnp.where(keep, ws_ref[p], jnp.zeros_like(ws_ref[p]))
            for blk in range(tm // SGU_BLOCK):
                rs = slice(blk * SGU_BLOCK, (blk + 1) * SGU_BLOCK)
                r = jnp.dot(wm, vn[rs, ls], preferred_element_type=F32)
                mixed = jnp.where(first_head, r[:SGU_BLOCK], r[SGU_BLOCK:]) + bs_ref[:, ls]
                ya_scr[part * tm + blk * SGU_BLOCK:part * tm + (blk + 1) * SGU_BLOCK, ls] = (
                    u[rs, ls] * mixed)
        yield

        ya = ya_scr[rows, :]
        sq = (ya * ya).astype(BF16)
        gw = gmat_ref.shape[0]
        ms = jnp.concatenate(
            [jnp.dot(sq[:, c * gw:(c + 1) * gw], gmat_ref[...], preferred_element_type=F32)
             for c in range(width // gw)], axis=1)
        ya_ref[rows, :] = (ya * lax.rsqrt(ms + EPS) * og_ref[...]).astype(BF16)

    @pl.when(i < n_tiles)
    def _():
        h2_of = {}
        for part in parts:
            h2_of[part] = ffn_stage(part, gate_stage(part))
        for part in parts:
            proj_stage(part, h2_of[part])

    @pl.when(i == n_tiles)
    def _():
        for part in parts:
            for _ in gate_stage(part):
                pass


def _fox_kernel(qt_ref, k_ref, fa_ref, vt_ref, og_ref, yb_ref,
                qaug_scr, m_scr, acc_scr, o_scr, stage_scr, *, n_heads):
    qi = pl.program_id(1)
    q_tiles, _, tm = qt_ref.shape
    tq = q_tiles * tm
    n_sub = tq // TQS
    blocks_per_tile = tq // TK
    sub_blocks = TQS // TK
    heads_per_group = LANES // HEAD_DIM

    rows = _iota((LANES, tm), 0)
    for h in range(n_heads):
        g, hh = divmod(h, heads_per_group)
        ones_part = jnp.where((rows < FORGET_PARTS * n_heads) & (lax.rem(rows, n_heads) == h),
                              1.0, 0.0).astype(BF16)
        for t in range(q_tiles):
            qt = qt_ref[t, g * LANES:(g + 1) * LANES, :]
            cols = slice(t * tm, (t + 1) * tm)
            qaug_scr[h, :LANES, cols] = jnp.where((rows // HEAD_DIM) == hh, qt, jnp.zeros_like(qt))
            qaug_scr[h, LANES:, cols] = ones_part
    m_scr[...] = jnp.full(m_scr.shape, NEG, F32)
    acc_scr[...] = jnp.zeros(acc_scr.shape, F32)

    ones_rows = jnp.ones((SUBLANES_BF16, TK), BF16)
    key_row = _iota((TK, TQS), 0)
    query_col = _iota((TK, TQS), 1)
    diag_masks = [key_row + d * TK <= query_col for d in range(sub_blocks)]

    def run(items):
        work = [(j, sub, mask, h) for (j, sub, mask) in items for h in range(n_heads)]
        loaded = {}

        def key_block(j, g):
            if (id(j), g) not in loaded:
                k0 = pl.multiple_of(j * TK, TK)
                loaded[(id(j), g)] = jnp.concatenate(
                    [k_ref[pl.ds(k0, TK), g * LANES:(g + 1) * LANES], fa_ref[pl.ds(k0, TK), :]],
                    axis=1)
            return loaded[(id(j), g)]

        def logits(w):
            j, sub, _, h = w
            return jnp.dot(key_block(j, h // heads_per_group),
                           qaug_scr[h, :, sub * TQS:(sub + 1) * TQS],
                           preferred_element_type=F32)

        def max_pass(i, s):
            _, sub, mask, h = work[i]
            if mask is not None:
                s = jnp.where(mask, s, NEG)
            slot = h * n_sub + sub
            m = m_scr[slot]
            m_new = jnp.maximum(m, jnp.max(s, axis=0, keepdims=True))
            m_scr[slot] = m_new
            stage_scr[i % STAGE_RING] = s
            return m_new, jnp.exp2(m - m_new)

        def exp_pass(i, m_new, alpha):
            j, sub, _, h = work[i]
            slot = h * n_sub + sub
            p = jnp.exp2(stage_scr[i % STAGE_RING] - m_new).astype(BF16)
            v_aug = jnp.concatenate([vt_ref[j, h * HEAD_DIM:(h + 1) * HEAD_DIM, :], ones_rows],
                                    axis=0)
            acc_scr[slot] = alpha * acc_scr[slot] + jnp.dot(v_aug, p, preferred_element_type=F32)

        n = len(work)
        pending = [logits(w) for w in work[:QK_LOOKAHEAD]]
        stats = {}
        for i in range(n + EXP_DELAY):
            if i + QK_LOOKAHEAD < n:
                pending.append(logits(work[i + QK_LOOKAHEAD]))
            if i < n:
                stats[i] = max_pass(i, pending.pop(0))
            if i >= EXP_DELAY:
                exp_pass(i - EXP_DELAY, *stats.pop(i - EXP_DELAY))

    first = qi * blocks_per_tile

    def body(t, carry):
        blocks = [t * KV_UNROLL + u for u in range(KV_UNROLL)]
        run([(j, sub, None) for j in blocks for sub in range(n_sub)])
        return carry

    lax.fori_loop(0, first // KV_UNROLL, body, 0)
    tail = []
    for b in range(blocks_per_tile):
        j = first + b
        for sub in range(n_sub):
            d = b - sub * sub_blocks
            if d < sub_blocks:
                tail.append((j, sub, diag_masks[d] if d >= 0 else None))
    run(tail)

    for h in range(n_heads):
        for sub in range(n_sub):
            acc = acc_scr[h * n_sub + sub]
            o = acc[:HEAD_DIM] / acc[HEAD_DIM:HEAD_DIM + 1]
            o_scr[h * HEAD_DIM:(h + 1) * HEAD_DIM, sub * TQS:(sub + 1) * TQS] = (
                o * lax.rsqrt(jnp.mean(o * o, axis=0, keepdims=True) + EPS))
    yb_ref[...] = (o_scr[...].T * og_ref[...]).astype(BF16)


def _back_kernel(x1_ref, ya_ref, yb_ref, wo_ref, g2_ref, w1_ref, w3_ref, w2_ref, gf_ref,
                 out_ref, g_scr, *, final_norm):
    tm = x1_ref.shape[0] // ROW_PARTS
    rows_of = [slice(part * tm, (part + 1) * tm) for part in range(ROW_PARTS)]
    x2_of = []
    for rows in rows_of:
        y = jnp.concatenate([ya_ref[rows, :], yb_ref[rows, :]], axis=1)
        x2_of.append(x1_ref[rows, :] + jnp.dot(y, wo_ref[...], preferred_element_type=F32))
    for rows, x2 in zip(rows_of, x2_of):
        h = _rmsnorm(x2, g2_ref[...]).astype(BF16)
        x3 = x2 + 0.5 * _swiglu(h, w1_ref, w3_ref, w2_ref, g_scr.at[rows, :])
        out_ref[rows, :] = _rmsnorm(x3, gf_ref[...]) if final_norm else x3


def _resident(shape):
    return pl.BlockSpec(shape, lambda *_: (0,) * len(shape), pipeline_mode=pl.Buffered(1))


def _row(v):
    return v.reshape(1, -1).astype(F32)


def _layer(xt, seq, final_norm, g1, w1a, w3a, w2a, gm, w_in, b_f, ln_g, ln_b, w_s, b_s, out_g,
           w_out, g2, w1b, w3b, w2b, gf):
    t_total, d = xt.shape
    d_ff = w1a.shape[1]
    n_heads = b_f.shape[0]
    width = n_heads * HEAD_DIM
    assert w_s.shape == (n_heads, SGU_BLOCK, SGU_BLOCK) and ln_g.shape == (width,)
    assert w_in.shape == (d, 5 * width + n_heads) and FORGET_PARTS * n_heads <= LANES
    assert seq % TQ == 0 and TQ % TM == 0 and TM % TQS == 0 and TM % TK == 0
    assert TQ % TQS == 0 and TQS % TK == 0 and (TQ // TK) % KV_UNROLL == 0
    assert STAGE_RING > EXP_DELAY
    assert (TM // ROW_PARTS) % SGU_BLOCK == 0 and (TM // ROW_PARTS) % TK == 0
    assert TM % SGU_BLOCK == 0 and d_ff % FF_CHUNK == 0 and width % (2 * LANES) == 0

    bf = lambda a: a.astype(BF16)
    wu, wv = bf(w_in[:, :width]), bf(w_in[:, width:2 * width])
    wq, wk = w_in[:, 2 * width:3 * width], bf(w_in[:, 3 * width:4 * width])
    wva = w_in[:, 4 * width:5 * width]
    f_rows = SUBLANES_BF16 * pl.cdiv(n_heads, SUBLANES_BF16)
    wf = jnp.pad(w_in[:, 5 * width:], ((0, 0), (0, f_rows - n_heads)))
    wqvft = bf(jnp.concatenate([wq, wva, wf], axis=1).T)
    bf_col = jnp.pad(b_f.astype(F32), (0, f_rows - n_heads)).reshape(f_rows, 1)
    ws_pairs = bf(w_s.reshape(n_heads // 2, 2 * SGU_BLOCK, SGU_BLOCK))
    bs_full = jnp.repeat(b_s.T.astype(F32), HEAD_DIM, axis=1)
    gw = 2 * LANES
    gmat = bf(jnp.where((jnp.arange(gw)[:, None] // HEAD_DIM) == (jnp.arange(gw)[None, :] // HEAD_DIM),
                        1.0 / HEAD_DIM, 0.0))
    og = _row(out_g)

    n_tiles = t_total // TM
    tile = lambda cols: pl.BlockSpec((TM, cols), lambda i: (i, 0))
    current = lambda i: jnp.minimum(i, n_tiles - 1)
    lagged = lambda i: jnp.maximum(i - 1, 0)
    cur = lambda cols: pl.BlockSpec((TM, cols), lambda i: (current(i), 0))
    lag = lambda cols: pl.BlockSpec((TM, cols), lambda i: (lagged(i), 0))
    front = pl.pallas_call(
        functools.partial(_front_kernel, tiles_per_seq=seq // TM, n_tiles=n_tiles,
                          scale=HEAD_DIM ** -0.5 * LOG2E),
        name="front",
        grid=(n_tiles + 1,),
        in_specs=[cur(d), _resident((1, d)), _resident((d, d_ff)), _resident((d, d_ff)),
                  _resident((d_ff, d)), _resident((1, d)), _resident((d, width)),
                  _resident((d, width)), _resident((d, width)),
                  _resident((2 * width + f_rows, d)), _resident((f_rows, 1)), _resident((1, width)),
                  _resident((1, width)), _resident(ws_pairs.shape), _resident((SGU_BLOCK, width)),
                  _resident((1, width)), _resident((gw, gw))],
        out_specs=[cur(d), lag(width), cur(width), lag(LANES),
                   pl.BlockSpec((1, width, TM), lambda i: (current(i), 0, 0)),
                   pl.BlockSpec((TM // TK, width, TK), lambda i: (current(i), 0, 0))],
        out_shape=[jax.ShapeDtypeStruct((t_total, d), F32),
                   jax.ShapeDtypeStruct((t_total, width), BF16),
                   jax.ShapeDtypeStruct((t_total, width), BF16),
                   jax.ShapeDtypeStruct((t_total, LANES), BF16),
                   jax.ShapeDtypeStruct((t_total // TM, width, TM), BF16),
                   jax.ShapeDtypeStruct((t_total // TK, width, TK), BF16)],
        scratch_shapes=[pltpu.VMEM((TM, d_ff), BF16), pltpu.VMEM((TM, width), F32),
                        pltpu.VMEM((f_rows, 1), F32), pltpu.VMEM((TM, 2 * width), F32),
                        pltpu.VMEM((f_rows, TM), F32)],
        compiler_params=pltpu.CompilerParams(dimension_semantics=("arbitrary",),
                                             vmem_limit_bytes=VMEM_LIMIT),
    )
    x1, ya, k, fa, qt, vt = front(
        xt, _row(g1), bf(w1a), bf(w3a), bf(w2a), _row(gm), wu, wv, wk, wqvft, bf_col,
        _row(ln_g), _row(ln_b), ws_pairs, bs_full, og[:, :width], gmat)

    n_batch = t_total // seq
    nq = seq // TQ
    fox = pl.pallas_call(
        functools.partial(_fox_kernel, n_heads=n_heads),
        name="fox",
        grid=(n_batch, nq),
        in_specs=[pl.BlockSpec((TQ // TM, width, TM), lambda b, q: (b * nq + q, 0, 0)),
                  pl.BlockSpec((seq, width), lambda b, q: (b, 0)),
                  pl.BlockSpec((seq, LANES), lambda b, q: (b, 0)),
                  pl.BlockSpec((seq // TK, width, TK), lambda b, q: (b, 0, 0)),
                  pl.BlockSpec((1, width), lambda b, q: (0, 0))],
        out_specs=pl.BlockSpec((TQ, width), lambda b, q: (b * nq + q, 0)),
        out_shape=jax.ShapeDtypeStruct((t_total, width), BF16),
        scratch_shapes=[pltpu.VMEM((n_heads, 2 * LANES, TQ), BF16),
                        pltpu.VMEM((n_heads * (TQ // TQS), 1, TQS), F32),
                        pltpu.VMEM((n_heads * (TQ // TQS), HEAD_DIM + SUBLANES_BF16, TQS), F32),
                        pltpu.VMEM((width, TQ), F32),
                        pltpu.VMEM((STAGE_RING, TK, TQS), F32)],
        compiler_params=pltpu.CompilerParams(
            dimension_semantics=("arbitrary", "arbitrary"),
            vmem_limit_bytes=VMEM_LIMIT),
    )
    yb = fox(qt, k, fa, vt, og[:, width:])

    back = pl.pallas_call(
        functools.partial(_back_kernel, final_norm=final_norm),
        name="back",
        grid=(n_tiles,),
        in_specs=[tile(d), tile(width), tile(width), _resident((2 * width, d)), _resident((1, d)),
                  _resident((d, d_ff)), _resident((d, d_ff)), _resident((d_ff, d)),
                  _resident((1, d))],
        out_specs=tile(d),
        out_shape=jax.ShapeDtypeStruct((t_total, d), F32),
        scratch_shapes=[pltpu.VMEM((TM, d_ff), BF16)],
        compiler_params=pltpu.CompilerParams(dimension_semantics=("arbitrary",),
                                             vmem_limit_bytes=VMEM_LIMIT),
    )
    return back(x1, ya, yb, bf(w_out), _row(g2), bf(w1b), bf(w3b), bf(w2b), _row(gf))


def kernel(x, ffn1_norm_g, ffn1_w1, ffn1_w3, ffn1_w2, mix_norm_g, w_in, fox_f_bias, sgu_ln_g, sgu_ln_b, sgu_w_s, sgu_b_s, mix_out_g, w_out, ffn2_norm_g, ffn2_w1, ffn2_w3, ffn2_w2, final_norm_g):
    n_batch, seq, d = x.shape
    depth = ffn1_w1.shape[0]
    xt = x.reshape(n_batch * seq, d)
    for l in range(depth):
        xt = _layer(xt, seq, l == depth - 1, ffn1_norm_g[l], ffn1_w1[l], ffn1_w3[l], ffn1_w2[l],
                    mix_norm_g[l], w_in[l], fox_f_bias[l], sgu_ln_g[l], sgu_ln_b[l], sgu_w_s[l],
                    sgu_b_s[l], mix_out_g[l], w_out[l], ffn2_norm_g[l], ffn2_w1[l], ffn2_w3[l],
                    ffn2_w2[l], final_norm_g)
    return xt.reshape(n_batch, seq, d)
```

```python
import functools

import jax
import jax.numpy as jnp
from jax import lax
from jax.experimental import pallas as pl
from jax.experimental.pallas import tpu as pltpu

F32 = jnp.float32
BF16 = jnp.bfloat16

HEAD_DIM = 64
CHUNK = 64
SGU_BLOCK = 128
EPS = 1e-6
LANES = 128
FORGET_PARTS = 3

TM = 512
ROW_PARTS = 2
SUBLANES_BF16 = 16
LOG2E = 1.4426950408889634

TQ = 1024
TQS = 256
TK = 256
KV_UNROLL = 4
FF_CHUNK = 256
QK_LOOKAHEAD = 5
EXP_DELAY = 3
STAGE_RING = 6
NEG = -1e30
VMEM_LIMIT = 58 * 1024 * 1024


def _iota(shape, dim):
    return lax.broadcasted_iota(jnp.int32, shape, dim)


def _rmsnorm(x, g):
    return x * lax.rsqrt(jnp.mean(x * x, axis=-1, keepdims=True) + EPS) * g


def _log_sigmoid(x):
    return jnp.minimum(x, 0.0) - jnp.log1p(jnp.exp(-jnp.abs(x)))


def _swiglu(h, w1_ref, w3_ref, w2_ref, g_scr, filler=None):
    d_ff = w1_ref.shape[1]
    for c in range(d_ff // FF_CHUNK):
        sl = slice(c * FF_CHUNK, (c + 1) * FF_CHUNK)
        a = jnp.dot(h, w1_ref[:, sl], preferred_element_type=F32)
        b = jnp.dot(h, w3_ref[:, sl], preferred_element_type=F32)
        g_scr[:, sl] = (a * jax.nn.sigmoid(a) * b).astype(BF16)
        if filler is not None:
            next(filler, None)
    if filler is not None:
        for _ in filler:
            pass
    return jnp.dot(g_scr[...], w2_ref[...], preferred_element_type=F32)


def _front_kernel(x_ref, g1_ref, w1_ref, w3_ref, w2_ref, gm_ref, wu_ref, wv_ref, wk_ref,
                  wqvft_ref, bf_ref, lng_ref, lnb_ref, ws_ref, bs_ref, og_ref,
                  gmat_ref, late_w1_ref, late_w3_ref, late_w2_ref, late_wo_ref,
                  x1_ref, ya_ref, k_ref, fa_ref, qt_ref, vt_ref,
                  bf_w1_ref, bf_w3_ref, bf_w2_ref, bf_wo_ref,
                  g_scr, ya_scr, carry_ref, z_scr, zft_scr, *, tiles_per_seq, n_tiles, scale):
    i = pl.program_id(0)
    tm = x_ref.shape[0] // ROW_PARTS
    width = wu_ref.shape[1]
    n_heads = width // HEAD_DIM

    @pl.when(i == 0)
    def _():
        z_scr[...] = jnp.zeros_like(z_scr)
        zft_scr[...] = jnp.zeros_like(zft_scr)

    @pl.when((i == 0) | (lax.rem(i + tiles_per_seq - 1, tiles_per_seq) == 0))
    def _():
        carry_ref[...] = jnp.zeros_like(carry_ref)

    pr = jnp.bitwise_and(_iota((2 * SGU_BLOCK, SGU_BLOCK), 0), SGU_BLOCK - 1)
    pc = _iota((2 * SGU_BLOCK, SGU_BLOCK), 1)
    keep = (pr // CHUNK) >= (pc // CHUNK)
    first_head = _iota((SGU_BLOCK, LANES), 1) < HEAD_DIM
    f_rows = zft_scr.shape[0]
    f_head = _iota((f_rows, tm), 0)
    f_token = _iota((f_rows, tm), 1)

    parts = range(ROW_PARTS)
    rows_of = [slice(part * tm, (part + 1) * tm) for part in parts]

    def ffn_stage(part, filler=None):
        rows = rows_of[part]
        x = x_ref[rows, :]
        h = _rmsnorm(x, g1_ref[...]).astype(BF16)
        x1 = x + 0.5 * _swiglu(h, w1_ref, w3_ref, w2_ref, g_scr.at[rows, :], filler)
        x1_ref[rows, :] = x1
        return _rmsnorm(x1, gm_ref[...]).astype(BF16)

    def proj_stage(part, h2):
        rows = rows_of[part]
        zt = lax.dot_general(wqvft_ref[...], h2, (((1,), (1,)), ((), ())),
                             preferred_element_type=F32)
        qt_ref[0, :, rows] = (zt[:width] * scale).astype(BF16)
        for t in range(tm // TK):
            vt_ref[part * (tm // TK) + t] = zt[width:2 * width, t * TK:(t + 1) * TK].astype(BF16)
        zft_scr[:, rows] = zt[2 * width:]
        k_ref[rows, :] = jnp.dot(h2, wk_ref[...], preferred_element_type=F32).astype(BF16)
        z_scr[rows, :width] = jnp.dot(h2, wu_ref[...], preferred_element_type=F32)
        z_scr[rows, width:] = jnp.dot(h2, wv_ref[...], preferred_element_type=F32)

    def gate_stage(part):
        rows = rows_of[part]
        zu = z_scr[rows, :width]
        zv = z_scr[rows, width:]
        fsum = jnp.where(f_head < n_heads, _log_sigmoid(zft_scr[:, rows] + bf_ref[...]), 0.0)
        shift = 1
        while shift < tm:
            fsum = fsum + jnp.where(f_token >= shift, pltpu.roll(fsum, shift, axis=1), 0.0)
            shift *= 2
        fsum = fsum + carry_ref[...]
        carry_ref[...] = fsum[:, tm - 1:tm]
        neg = fsum * (-LOG2E)
        hi = neg.astype(BF16)
        rem = neg - hi.astype(F32)
        mid = rem.astype(BF16)
        lo = (rem - mid.astype(F32)).astype(BF16)
        f_parts = jnp.concatenate(
            [part_t.astype(F32)[:n_heads] for part_t in (hi, mid, lo)]
            + [jnp.zeros((LANES - FORGET_PARTS * n_heads, tm), F32)], axis=0)
        fa_ref[rows, :] = f_parts.T.astype(BF16)
        yield

        u = jax.nn.gelu(zu)
        yield
        gv = jax.nn.gelu(zv)
        mu = jnp.mean(gv, axis=-1, keepdims=True)
        xc = gv - mu
        var = jnp.mean(xc * xc, axis=-1, keepdims=True)
        vn = (xc * lax.rsqrt(var + EPS) * lng_ref[...] + lnb_ref[...]).astype(BF16)
        yield
        yield
        for p in range(width // LANES):
            if p == width // LANES // 2:
                yield
            ls = slice(p * LANES, (p + 1) * LANES)
            wm = jnp.where(keep, ws_ref[p], jnp.zeros_like(ws_ref[p]))
            for blk in range(tm // SGU_BLOCK):
                rs = slice(blk * SGU_BLOCK, (blk + 1) * SGU_BLOCK)
                r = jnp.dot(wm, vn[rs, ls], preferred_element_type=F32)
                mixed = jnp.where(first_head, r[:SGU_BLOCK], r[SGU_BLOCK:]) + bs_ref[:, ls]
                ya_scr[part * tm + blk * SGU_BLOCK:part * tm + (blk + 1) * SGU_BLOCK, ls] = (
                    u[rs, ls] * mixed)
        yield

        ya = ya_scr[rows, :]
        sq = (ya * ya).astype(BF16)
        gw = gmat_ref.shape[0]
        ms = jnp.concatenate(
            [jnp.dot(sq[:, c * gw:(c + 1) * gw], gmat_ref[...], preferred_element_type=F32)
             for c in range(width // gw)], axis=1)
        ya_ref[rows, :] = (ya * lax.rsqrt(ms + EPS) * og_ref[...]).astype(BF16)

    @pl.when(i < n_tiles)
    def _():
        h2_of = {}
        for part in parts:
            h2_of[part] = ffn_stage(part, gate_stage(part))
        for part in parts:
            proj_stage(part, h2_of[part])
        for src, dst in ((late_w1_ref, bf_w1_ref), (late_w3_ref, bf_w3_ref),
                         (late_w2_ref, bf_w2_ref), (late_wo_ref, bf_wo_ref)):
            dst[...] = src[...].astype(BF16)

    @pl.when(i == n_tiles)
    def _():
        for part in parts:
            for _ in gate_stage(part):
                pass


def _fox_kernel(qt_ref, k_ref, fa_ref, vt_ref, og_ref, yb_ref,
                qaug_scr, m_scr, acc_scr, o_scr, stage_scr, *, n_heads):
    qi = pl.program_id(1)
    q_tiles, _, tm = qt_ref.shape
    tq = q_tiles * tm
    n_sub = tq // TQS
    blocks_per_tile = tq // TK
    sub_blocks = TQS // TK
    heads_per_group = LANES // HEAD_DIM

    rows = _iota((LANES, tm), 0)
    for h in range(n_heads):
        g, hh = divmod(h, heads_per_group)
        ones_part = jnp.where((rows < FORGET_PARTS * n_heads) & (lax.rem(rows, n_heads) == h),
                              1.0, 0.0).astype(BF16)
        for t in range(q_tiles):
            qt = qt_ref[t, g * LANES:(g + 1) * LANES, :]
            cols = slice(t * tm, (t + 1) * tm)
            qaug_scr[h, :LANES, cols] = jnp.where((rows // HEAD_DIM) == hh, qt, jnp.zeros_like(qt))
            qaug_scr[h, LANES:, cols] = ones_part
    m_scr[...] = jnp.full(m_scr.shape, NEG, F32)
    acc_scr[...] = jnp.zeros(acc_scr.shape, F32)

    ones_rows = jnp.ones((SUBLANES_BF16, TK), BF16)
    key_row = _iota((TK, TQS), 0)
    query_col = _iota((TK, TQS), 1)
    diag_masks = [key_row + d * TK <= query_col for d in range(sub_blocks)]

    def run(items):
        work = [(j, sub, mask, h) for (j, sub, mask) in items for h in range(n_heads)]
        loaded = {}

        def key_block(j, g):
            if (id(j), g) not in loaded:
                k0 = pl.multiple_of(j * TK, TK)
                loaded[(id(j), g)] = jnp.concatenate(
                    [k_ref[pl.ds(k0, TK), g * LANES:(g + 1) * LANES], fa_ref[pl.ds(k0, TK), :]],
                    axis=1)
            return loaded[(id(j), g)]

        def logits(w):
            j, sub, _, h = w
            return jnp.dot(key_block(j, h // heads_per_group),
                           qaug_scr[h, :, sub * TQS:(sub + 1) * TQS],
                           preferred_element_type=F32)

        def max_pass(i, s):
            _, sub, mask, h = work[i]
            if mask is not None:
                s = jnp.where(mask, s, NEG)
            slot = h * n_sub + sub
            m = m_scr[slot]
            m_new = jnp.maximum(m, jnp.max(s, axis=0, keepdims=True))
            m_scr[slot] = m_new
            stage_scr[i % STAGE_RING] = s
            return m_new, jnp.exp2(m - m_new)

        def exp_pass(i, m_new, alpha):
            j, sub, _, h = work[i]
            slot = h * n_sub + sub
            p = jnp.exp2(stage_scr[i % STAGE_RING] - m_new).astype(BF16)
            v_aug = jnp.concatenate([vt_ref[j, h * HEAD_DIM:(h + 1) * HEAD_DIM, :], ones_rows],
                                    axis=0)
            acc_scr[slot] = alpha * acc_scr[slot] + jnp.dot(v_aug, p, preferred_element_type=F32)

        n = len(work)
        pending = [logits(w) for w in work[:QK_LOOKAHEAD]]
        stats = {}
        for i in range(n + EXP_DELAY):
            if i + QK_LOOKAHEAD < n:
                pending.append(logits(work[i + QK_LOOKAHEAD]))
            if i < n:
                stats[i] = max_pass(i, pending.pop(0))
            if i >= EXP_DELAY:
                exp_pass(i - EXP_DELAY, *stats.pop(i - EXP_DELAY))

    first = qi * blocks_per_tile

    def body(t, carry):
        blocks = [t * KV_UNROLL + u for u in range(KV_UNROLL)]
        run([(j, sub, None) for j in blocks for sub in range(n_sub)])
        return carry

    lax.fori_loop(0, first // KV_UNROLL, body, 0)
    tail = []
    for b in range(blocks_per_tile):
        j = first + b
        for sub in range(n_sub):
            d = b - sub * sub_blocks
            if d < sub_blocks:
                tail.append((j, sub, diag_masks[d] if d >= 0 else None))
    run(tail)

    for h in range(n_heads):
        for sub in range(n_sub):
            acc = acc_scr[h * n_sub + sub]
            o = acc[:HEAD_DIM] / acc[HEAD_DIM:HEAD_DIM + 1]
            o_scr[h * HEAD_DIM:(h + 1) * HEAD_DIM, sub * TQS:(sub + 1) * TQS] = (
                o * lax.rsqrt(jnp.mean(o * o, axis=0, keepdims=True) + EPS))
    yb_ref[...] = (o_scr[...].T * og_ref[...]).astype(BF16)


def _back_kernel(x1_ref, ya_ref, yb_ref, wo_ref, g2_ref, w1_ref, w3_ref, w2_ref, gf_ref,
                 out_ref, g_scr, *, final_norm):
    tm = x1_ref.shape[0] // ROW_PARTS
    rows_of = [slice(part * tm, (part + 1) * tm) for part in range(ROW_PARTS)]
    x2_of = []
    for rows in rows_of:
        y = jnp.concatenate([ya_ref[rows, :], yb_ref[rows, :]], axis=1)
        x2_of.append(x1_ref[rows, :] + jnp.dot(y, wo_ref[...], preferred_element_type=F32))
    for rows, x2 in zip(rows_of, x2_of):
        h = _rmsnorm(x2, g2_ref[...]).astype(BF16)
        x3 = x2 + 0.5 * _swiglu(h, w1_ref, w3_ref, w2_ref, g_scr.at[rows, :])
        out_ref[rows, :] = _rmsnorm(x3, gf_ref[...]) if final_norm else x3


def _resident(shape):
    return pl.BlockSpec(shape, lambda *_: (0,) * len(shape), pipeline_mode=pl.Buffered(1))


def _row(v):
    return v.reshape(1, -1).astype(F32)


def _layer(xt, seq, final_norm, g1, w1a, w3a, w2a, gm, w_in, b_f, ln_g, ln_b, w_s, b_s, out_g,
           w_out, g2, w1b, w3b, w2b, gf):
    t_total, d = xt.shape
    d_ff = w1a.shape[1]
    n_heads = b_f.shape[0]
    width = n_heads * HEAD_DIM
    assert w_s.shape == (n_heads, SGU_BLOCK, SGU_BLOCK) and ln_g.shape == (width,)
    assert w_in.shape == (d, 5 * width + n_heads) and FORGET_PARTS * n_heads <= LANES
    assert seq % TQ == 0 and TQ % TM == 0 and TM % TQS == 0 and TM % TK == 0
    assert TQ % TQS == 0 and TQS % TK == 0 and (TQ // TK) % KV_UNROLL == 0
    assert STAGE_RING > EXP_DELAY
    assert (TM // ROW_PARTS) % SGU_BLOCK == 0 and (TM // ROW_PARTS) % TK == 0
    assert TM % SGU_BLOCK == 0 and d_ff % FF_CHUNK == 0 and width % (2 * LANES) == 0

    bf = lambda a: a.astype(BF16)
    wu, wv = bf(w_in[:, :width]), bf(w_in[:, width:2 * width])
    wq, wk = w_in[:, 2 * width:3 * width], bf(w_in[:, 3 * width:4 * width])
    wva = w_in[:, 4 * width:5 * width]
    f_rows = SUBLANES_BF16 * pl.cdiv(n_heads, SUBLANES_BF16)
    wf = jnp.pad(w_in[:, 5 * width:], ((0, 0), (0, f_rows - n_heads)))
    wqvft = bf(jnp.concatenate([wq, wva, wf], axis=1).T)
    bf_col = jnp.pad(b_f.astype(F32), (0, f_rows - n_heads)).reshape(f_rows, 1)
    ws_pairs = bf(w_s.reshape(n_heads // 2, 2 * SGU_BLOCK, SGU_BLOCK))
    bs_full = jnp.repeat(b_s.T.astype(F32), HEAD_DIM, axis=1)
    gw = 2 * LANES
    gmat = bf(jnp.where((jnp.arange(gw)[:, None] // HEAD_DIM) == (jnp.arange(gw)[None, :] // HEAD_DIM),
                        1.0 / HEAD_DIM, 0.0))
    og = _row(out_g)

    n_tiles = t_total // TM
    tile = lambda cols: pl.BlockSpec((TM, cols), lambda i: (i, 0))
    current = lambda i: jnp.minimum(i, n_tiles - 1)
    lagged = lambda i: jnp.maximum(i - 1, 0)
    cur = lambda cols: pl.BlockSpec((TM, cols), lambda i: (current(i), 0))
    lag = lambda cols: pl.BlockSpec((TM, cols), lambda i: (lagged(i), 0))
    late = [w1b, w3b, w2b.reshape(d, d_ff), w_out]
    assert all(w.shape[0] == d for w in late) and d % (n_tiles * SUBLANES_BF16) == 0
    slab = lambda w: pl.BlockSpec((d // n_tiles, w.shape[1]), lambda i: (current(i), 0))
    front = pl.pallas_call(
        functools.partial(_front_kernel, tiles_per_seq=seq // TM, n_tiles=n_tiles,
                          scale=HEAD_DIM ** -0.5 * LOG2E),
        name="front",
        grid=(n_tiles + 1,),
        in_specs=[cur(d), _resident((1, d)), _resident((d, d_ff)), _resident((d, d_ff)),
                  _resident((d_ff, d)), _resident((1, d)), _resident((d, width)),
                  _resident((d, width)), _resident((d, width)),
                  _resident((2 * width + f_rows, d)), _resident((f_rows, 1)), _resident((1, width)),
                  _resident((1, width)), _resident(ws_pairs.shape), _resident((SGU_BLOCK, width)),
                  _resident((1, width)), _resident((gw, gw))] + [slab(w) for w in late],
        out_specs=[cur(d), lag(width), cur(width), lag(LANES),
                   pl.BlockSpec((1, width, TM), lambda i: (current(i), 0, 0)),
                   pl.BlockSpec((TM // TK, width, TK), lambda i: (current(i), 0, 0))]
        + [slab(w) for w in late],
        out_shape=[jax.ShapeDtypeStruct((t_total, d), F32),
                   jax.ShapeDtypeStruct((t_total, width), BF16),
                   jax.ShapeDtypeStruct((t_total, width), BF16),
                   jax.ShapeDtypeStruct((t_total, LANES), BF16),
                   jax.ShapeDtypeStruct((t_total // TM, width, TM), BF16),
                   jax.ShapeDtypeStruct((t_total // TK, width, TK), BF16)]
        + [jax.ShapeDtypeStruct(w.shape, BF16) for w in late],
        scratch_shapes=[pltpu.VMEM((TM, d_ff), BF16), pltpu.VMEM((TM, width), F32),
                        pltpu.VMEM((f_rows, 1), F32), pltpu.VMEM((TM, 2 * width), F32),
                        pltpu.VMEM((f_rows, TM), F32)],
        compiler_params=pltpu.CompilerParams(dimension_semantics=("arbitrary",),
                                             vmem_limit_bytes=VMEM_LIMIT),
    )
    x1, ya, k, fa, qt, vt, w1b_bf, w3b_bf, w2b_bf, wo_bf = front(
        xt, _row(g1), bf(w1a), bf(w3a), bf(w2a), _row(gm), wu, wv, wk, wqvft, bf_col,
        _row(ln_g), _row(ln_b), ws_pairs, bs_full, og[:, :width], gmat, *late)
    w2b_bf = w2b_bf.reshape(w2b.shape)

    n_batch = t_total // seq
    nq = seq // TQ
    fox = pl.pallas_call(
        functools.partial(_fox_kernel, n_heads=n_heads),
        name="fox",
        grid=(n_batch, nq),
        in_specs=[pl.BlockSpec((TQ // TM, width, TM), lambda b, q: (b * nq + q, 0, 0)),
                  pl.BlockSpec((seq, width), lambda b, q: (b, 0)),
                  pl.BlockSpec((seq, LANES), lambda b, q: (b, 0)),
                  pl.BlockSpec((seq // TK, width, TK), lambda b, q: (b, 0, 0)),
                  pl.BlockSpec((1, width), lambda b, q: (0, 0))],
        out_specs=pl.BlockSpec((TQ, width), lambda b, q: (b * nq + q, 0)),
        out_shape=jax.ShapeDtypeStruct((t_total, width), BF16),
        scratch_shapes=[pltpu.VMEM((n_heads, 2 * LANES, TQ), BF16),
                        pltpu.VMEM((n_heads * (TQ // TQS), 1, TQS), F32),
                        pltpu.VMEM((n_heads * (TQ // TQS), HEAD_DIM + SUBLANES_BF16, TQS), F32),
                        pltpu.VMEM((width, TQ), F32),
                        pltpu.VMEM((STAGE_RING, TK, TQS), F32)],
        compiler_params=pltpu.CompilerParams(
            dimension_semantics=("arbitrary", "arbitrary"),
            vmem_limit_bytes=VMEM_LIMIT),
    )
    yb = fox(qt, k, fa, vt, og[:, width:])

    back = pl.pallas_call(
        functools.partial(_back_kernel, final_norm=final_norm),
        name="back",
        grid=(n_tiles,),
        in_specs=[tile(d), tile(width), tile(width), _resident((2 * width, d)), _resident((1, d)),
                  _resident((d, d_ff)), _resident((d, d_ff)), _resident((d_ff, d)),
                  _resident((1, d))],
        out_specs=tile(d),
        out_shape=jax.ShapeDtypeStruct((t_total, d), F32),
        scratch_shapes=[pltpu.VMEM((TM, d_ff), BF16)],
        compiler_params=pltpu.CompilerParams(dimension_semantics=("arbitrary",),
                                             vmem_limit_bytes=VMEM_LIMIT),
    )
    return back(x1, ya, yb, wo_bf, _row(g2), w1b_bf, w3b_bf, w2b_bf, _row(gf))


def kernel(x, ffn1_norm_g, ffn1_w1, ffn1_w3, ffn1_w2, mix_norm_g, w_in, fox_f_bias, sgu_ln_g, sgu_ln_b, sgu_w_s, sgu_b_s, mix_out_g, w_out, ffn2_norm_g, ffn2_w1, ffn2_w3, ffn2_w2, final_norm_g):
    n_batch, seq, d = x.shape
    depth = ffn1_w1.shape[0]
    xt = x.reshape(n_batch * seq, d)
    for l in range(depth):
        xt = _layer(xt, seq, l == depth - 1, ffn1_norm_g[l], ffn1_w1[l], ffn1_w3[l], ffn1_w2[l],
                    mix_norm_g[l], w_in[l], fox_f_bias[l], sgu_ln_g[l], sgu_ln_b[l], sgu_w_s[l],
                    sgu_b_s[l], mix_out_g[l], w_out[l], ffn2_norm_g[l], ffn2_w1[l], ffn2_w3[l],
                    ffn2_w2[l], final_norm_g)
    return xt.reshape(n_batch, seq, d)
```

```python
import functools

import jax
import jax.numpy as jnp
from jax import lax
from jax.experimental import pallas as pl
from jax.experimental.pallas import tpu as pltpu

F32 = jnp.float32
BF16 = jnp.bfloat16

HEAD_DIM = 64
CHUNK = 64
SGU_BLOCK = 128
EPS = 1e-6
LANES = 128
FORGET_PARTS = 3

TM = 512
ROW_PARTS = 2
SUBLANES_BF16 = 16
LOG2E = 1.4426950408889634

TQ = 1024
TQS = 256
TK = 256
KV_UNROLL = 4
FF_CHUNK = 256
QK_LOOKAHEAD = 5
EXP_DELAY = 3
STAGE_RING = 6
NEG = -1e30
VMEM_LIMIT = 58 * 1024 * 1024


def _iota(shape, dim):
    return lax.broadcasted_iota(jnp.int32, shape, dim)


def _rmsnorm(x, g):
    return x * lax.rsqrt(jnp.mean(x * x, axis=-1, keepdims=True) + EPS) * g


def _log_sigmoid(x):
    return jnp.minimum(x, 0.0) - jnp.log1p(jnp.exp(-jnp.abs(x)))


def _swiglu(h, w1_ref, w3_ref, w2_ref, g_scr, filler=None):
    d_ff = w1_ref.shape[1]
    for c in range(d_ff // FF_CHUNK):
        sl = slice(c * FF_CHUNK, (c + 1) * FF_CHUNK)
        a = jnp.dot(h, w1_ref[:, sl], preferred_element_type=F32)
        b = jnp.dot(h, w3_ref[:, sl], preferred_element_type=F32)
        g_scr[:, sl] = (a * jax.nn.sigmoid(a) * b).astype(BF16)
        if filler is not None:
            next(filler, None)
    if filler is not None:
        for _ in filler:
            pass
    return jnp.dot(g_scr[...], w2_ref[...], preferred_element_type=F32)


def _front_kernel(x_ref, g1_ref, w1_ref, w3_ref, w2_ref, gm_ref, wu_ref, wv_ref, wk_ref,
                  wqvft_ref, bf_ref, lng_ref, lnb_ref, ws_ref, bs_ref, og_ref,
                  gmat_ref, late_w1_ref, late_w3_ref, late_wo_ref,
                  x1_ref, ya_ref, k_ref, fa_ref, qt_ref, vt_ref,
                  bf_w1_ref, bf_w3_ref, bf_wo_ref,
                  g_scr, ya_scr, carry_ref, z_scr, zft_scr, *, tiles_per_seq, n_tiles, scale):
    i = pl.program_id(0)
    tm = x_ref.shape[0] // ROW_PARTS
    width = wu_ref.shape[1]
    n_heads = width // HEAD_DIM

    @pl.when(i == 0)
    def _():
        z_scr[...] = jnp.zeros_like(z_scr)
        zft_scr[...] = jnp.zeros_like(zft_scr)

    @pl.when((i == 0) | (lax.rem(i + tiles_per_seq - 1, tiles_per_seq) == 0))
    def _():
        carry_ref[...] = jnp.zeros_like(carry_ref)

    pr = jnp.bitwise_and(_iota((2 * SGU_BLOCK, SGU_BLOCK), 0), SGU_BLOCK - 1)
    pc = _iota((2 * SGU_BLOCK, SGU_BLOCK), 1)
    keep = (pr // CHUNK) >= (pc // CHUNK)
    first_head = _iota((SGU_BLOCK, LANES), 1) < HEAD_DIM
    f_rows = zft_scr.shape[0]
    f_head = _iota((f_rows, tm), 0)
    f_token = _iota((f_rows, tm), 1)

    parts = range(ROW_PARTS)
    rows_of = [slice(part * tm, (part + 1) * tm) for part in parts]

    def ffn_stage(part, filler=None):
        rows = rows_of[part]
        x = x_ref[rows, :]
        h = _rmsnorm(x, g1_ref[...]).astype(BF16)
        x1 = x + 0.5 * _swiglu(h, w1_ref, w3_ref, w2_ref, g_scr.at[rows, :], filler)
        x1_ref[rows, :] = x1
        return _rmsnorm(x1, gm_ref[...]).astype(BF16)

    def proj_stage(part, h2):
        rows = rows_of[part]
        zt = lax.dot_general(wqvft_ref[...], h2, (((1,), (1,)), ((), ())),
                             preferred_element_type=F32)
        qt_ref[0, :, rows] = (zt[:width] * scale).astype(BF16)
        for t in range(tm // TK):
            vt_ref[part * (tm // TK) + t] = zt[width:2 * width, t * TK:(t + 1) * TK].astype(BF16)
        zft_scr[:, rows] = zt[2 * width:]
        k_ref[rows, :] = jnp.dot(h2, wk_ref[...], preferred_element_type=F32).astype(BF16)
        z_scr[rows, :width] = jnp.dot(h2, wu_ref[...], preferred_element_type=F32)
        z_scr[rows, width:] = jnp.dot(h2, wv_ref[...], preferred_element_type=F32)

    def gate_stage(part):
        rows = rows_of[part]
        zu = z_scr[rows, :width]
        zv = z_scr[rows, width:]
        fsum = jnp.where(f_head < n_heads, _log_sigmoid(zft_scr[:, rows] + bf_ref[...]), 0.0)
        shift = 1
        while shift < tm:
            fsum = fsum + jnp.where(f_token >= shift, pltpu.roll(fsum, shift, axis=1), 0.0)
            shift *= 2
        fsum = fsum + carry_ref[...]
        carry_ref[...] = fsum[:, tm - 1:tm]
        neg = fsum * (-LOG2E)
        hi = neg.astype(BF16)
        rem = neg - hi.astype(F32)
        mid = rem.astype(BF16)
        lo = (rem - mid.astype(F32)).astype(BF16)
        f_parts = jnp.concatenate(
            [part_t.astype(F32)[:n_heads] for part_t in (hi, mid, lo)]
            + [jnp.zeros((LANES - FORGET_PARTS * n_heads, tm), F32)], axis=0)
        fa_ref[rows, :] = f_parts.T.astype(BF16)
        yield

        u = jax.nn.gelu(zu)
        yield
        gv = jax.nn.gelu(zv)
        mu = jnp.mean(gv, axis=-1, keepdims=True)
        xc = gv - mu
        var = jnp.mean(xc * xc, axis=-1, keepdims=True)
        vn = (xc * lax.rsqrt(var + EPS) * lng_ref[...] + lnb_ref[...]).astype(BF16)
        yield
        yield
        for p in range(width // LANES):
            if p == width // LANES // 2:
                yield
            ls = slice(p * LANES, (p + 1) * LANES)
            wm = jnp.where(keep, ws_ref[p], jnp.zeros_like(ws_ref[p]))
            for blk in range(tm // SGU_BLOCK):
                rs = slice(blk * SGU_BLOCK, (blk + 1) * SGU_BLOCK)
                r = jnp.dot(wm, vn[rs, ls], preferred_element_type=F32)
                mixed = jnp.where(first_head, r[:SGU_BLOCK], r[SGU_BLOCK:]) + bs_ref[:, ls]
                ya_scr[part * tm + blk * SGU_BLOCK:part * tm + (blk + 1) * SGU_BLOCK, ls] = (
                    u[rs, ls] * mixed)
        yield

        ya = ya_scr[rows, :]
        sq = (ya * ya).astype(BF16)
        gw = gmat_ref.shape[0]
        ms = jnp.concatenate(
            [jnp.dot(sq[:, c * gw:(c + 1) * gw], gmat_ref[...], preferred_element_type=F32)
             for c in range(width // gw)], axis=1)
        ya_ref[rows, :] = (ya * lax.rsqrt(ms + EPS) * og_ref[...]).astype(BF16)

    @pl.when(i < n_tiles)
    def _():
        h2_of = {}
        for part in parts:
            h2_of[part] = ffn_stage(part, gate_stage(part))
        for part in parts:
            proj_stage(part, h2_of[part])
        for src, dst in ((late_w1_ref, bf_w1_ref), (late_w3_ref, bf_w3_ref),
                         (late_wo_ref, bf_wo_ref)):
            dst[...] = src[...].astype(BF16)

    @pl.when(i == n_tiles)
    def _():
        for part in parts:
            for _ in gate_stage(part):
                pass


def _fox_kernel(qt_ref, k_ref, fa_ref, vt_ref, og_ref, yb_ref,
                qaug_scr, m_scr, acc_scr, o_scr, stage_scr, *, n_heads):
    qi = pl.program_id(1)
    q_tiles, _, tm = qt_ref.shape
    tq = q_tiles * tm
    n_sub = tq // TQS
    blocks_per_tile = tq // TK
    sub_blocks = TQS // TK
    heads_per_group = LANES // HEAD_DIM

    rows = _iota((LANES, tm), 0)
    for h in range(n_heads):
        g, hh = divmod(h, heads_per_group)
        ones_part = jnp.where((rows < FORGET_PARTS * n_heads) & (lax.rem(rows, n_heads) == h),
                              1.0, 0.0).astype(BF16)
        for t in range(q_tiles):
            qt = qt_ref[t, g * LANES:(g + 1) * LANES, :]
            cols = slice(t * tm, (t + 1) * tm)
            qaug_scr[h, :LANES, cols] = jnp.where((rows // HEAD_DIM) == hh, qt, jnp.zeros_like(qt))
            qaug_scr[h, LANES:, cols] = ones_part
    m_scr[...] = jnp.full(m_scr.shape, NEG, F32)
    acc_scr[...] = jnp.zeros(acc_scr.shape, F32)

    ones_rows = jnp.ones((SUBLANES_BF16, TK), BF16)
    key_row = _iota((TK, TQS), 0)
    query_col = _iota((TK, TQS), 1)
    diag_masks = [key_row + d * TK <= query_col for d in range(sub_blocks)]

    def run(items):
        work = [(j, sub, mask, h) for (j, sub, mask) in items for h in range(n_heads)]
        loaded = {}

        def key_block(j, g):
            if (id(j), g) not in loaded:
                k0 = pl.multiple_of(j * TK, TK)
                loaded[(id(j), g)] = jnp.concatenate(
                    [k_ref[pl.ds(k0, TK), g * LANES:(g + 1) * LANES], fa_ref[pl.ds(k0, TK), :]],
                    axis=1)
            return loaded[(id(j), g)]

        def logits(w):
            j, sub, _, h = w
            return jnp.dot(key_block(j, h // heads_per_group),
                           qaug_scr[h, :, sub * TQS:(sub + 1) * TQS],
                           preferred_element_type=F32)

        def max_pass(i, s):
            _, sub, mask, h = work[i]
            if mask is not None:
                s = jnp.where(mask, s, NEG)
            slot = h * n_sub + sub
            m = m_scr[slot]
            m_new = jnp.maximum(m, jnp.max(s, axis=0, keepdims=True))
            m_scr[slot] = m_new
            stage_scr[i % STAGE_RING] = s
            return m_new, jnp.exp2(m - m_new)

        def exp_pass(i, m_new, alpha):
            j, sub, _, h = work[i]
            slot = h * n_sub + sub
            p = jnp.exp2(stage_scr[i % STAGE_RING] - m_new).astype(BF16)
            v_aug = jnp.concatenate([vt_ref[j, h * HEAD_DIM:(h + 1) * HEAD_DIM, :], ones_rows],
                                    axis=0)
            acc_scr[slot] = alpha * acc_scr[slot] + jnp.dot(v_aug, p, preferred_element_type=F32)

        n = len(work)
        pending = [logits(w) for w in work[:QK_LOOKAHEAD]]
        stats = {}
        for i in range(n + EXP_DELAY):
            if i + QK_LOOKAHEAD < n:
                pending.append(logits(work[i + QK_LOOKAHEAD]))
            if i < n:
                stats[i] = max_pass(i, pending.pop(0))
            if i >= EXP_DELAY:
                exp_pass(i - EXP_DELAY, *stats.pop(i - EXP_DELAY))

    first = qi * blocks_per_tile

    def body(t, carry):
        blocks = [t * KV_UNROLL + u for u in range(KV_UNROLL)]
        run([(j, sub, None) for j in blocks for sub in range(n_sub)])
        return carry

    lax.fori_loop(0, first // KV_UNROLL, body, 0)
    tail = []
    for b in range(blocks_per_tile):
        j = first + b
        for sub in range(n_sub):
            d = b - sub * sub_blocks
            if d < sub_blocks:
                tail.append((j, sub, diag_masks[d] if d >= 0 else None))
    run(tail)

    for h in range(n_heads):
        for sub in range(n_sub):
            acc = acc_scr[h * n_sub + sub]
            o = acc[:HEAD_DIM] / acc[HEAD_DIM:HEAD_DIM + 1]
            o_scr[h * HEAD_DIM:(h + 1) * HEAD_DIM, sub * TQS:(sub + 1) * TQS] = (
                o * lax.rsqrt(jnp.mean(o * o, axis=0, keepdims=True) + EPS))
    yb_ref[...] = (o_scr[...].T * og_ref[...]).astype(BF16)


def _back_kernel(x1_ref, ya_ref, yb_ref, wo_ref, g2_ref, w1_ref, w3_ref, w2_ref, gf_ref,
                 out_ref, g_scr, *, final_norm):
    tm = x1_ref.shape[0] // ROW_PARTS
    rows_of = [slice(part * tm, (part + 1) * tm) for part in range(ROW_PARTS)]
    x2_of = []
    for rows in rows_of:
        y = jnp.concatenate([ya_ref[rows, :], yb_ref[rows, :]], axis=1)
        x2_of.append(x1_ref[rows, :] + jnp.dot(y, wo_ref[...], preferred_element_type=F32))
    for rows, x2 in zip(rows_of, x2_of):
        h = _rmsnorm(x2, g2_ref[...]).astype(BF16)
        x3 = x2 + 0.5 * _swiglu(h, w1_ref, w3_ref, w2_ref, g_scr.at[rows, :])
        out_ref[rows, :] = _rmsnorm(x3, gf_ref[...]) if final_norm else x3


def _resident(shape):
    return pl.BlockSpec(shape, lambda *_: (0,) * len(shape), pipeline_mode=pl.Buffered(1))


def _row(v):
    return v.reshape(1, -1).astype(F32)


def _layer(xt, seq, final_norm, g1, w1a, w3a, w2a, gm, w_in, b_f, ln_g, ln_b, w_s, b_s, out_g,
           w_out, g2, w1b, w3b, w2b, gf):
    t_total, d = xt.shape
    d_ff = w1a.shape[1]
    n_heads = b_f.shape[0]
    width = n_heads * HEAD_DIM
    assert w_s.shape == (n_heads, SGU_BLOCK, SGU_BLOCK) and ln_g.shape == (width,)
    assert w_in.shape == (d, 5 * width + n_heads) and FORGET_PARTS * n_heads <= LANES
    assert seq % TQ == 0 and TQ % TM == 0 and TM % TQS == 0 and TM % TK == 0
    assert TQ % TQS == 0 and TQS % TK == 0 and (TQ // TK) % KV_UNROLL == 0
    assert STAGE_RING > EXP_DELAY
    assert (TM // ROW_PARTS) % SGU_BLOCK == 0 and (TM // ROW_PARTS) % TK == 0
    assert TM % SGU_BLOCK == 0 and d_ff % FF_CHUNK == 0 and width % (2 * LANES) == 0

    bf = lambda a: a.astype(BF16)
    wu, wv = bf(w_in[:, :width]), bf(w_in[:, width:2 * width])
    wq, wk = w_in[:, 2 * width:3 * width], bf(w_in[:, 3 * width:4 * width])
    wva = w_in[:, 4 * width:5 * width]
    f_rows = SUBLANES_BF16 * pl.cdiv(n_heads, SUBLANES_BF16)
    wf = jnp.pad(w_in[:, 5 * width:], ((0, 0), (0, f_rows - n_heads)))
    wqvft = bf(jnp.concatenate([wq, wva, wf], axis=1).T)
    bf_col = jnp.pad(b_f.astype(F32), (0, f_rows - n_heads)).reshape(f_rows, 1)
    ws_pairs = bf(w_s.reshape(n_heads // 2, 2 * SGU_BLOCK, SGU_BLOCK))
    bs_full = jnp.repeat(b_s.T.astype(F32), HEAD_DIM, axis=1)
    gw = 2 * LANES
    gmat = bf(jnp.where((jnp.arange(gw)[:, None] // HEAD_DIM) == (jnp.arange(gw)[None, :] // HEAD_DIM),
                        1.0 / HEAD_DIM, 0.0))
    og = _row(out_g)

    n_tiles = t_total // TM
    tile = lambda cols: pl.BlockSpec((TM, cols), lambda i: (i, 0))
    current = lambda i: jnp.minimum(i, n_tiles - 1)
    lagged = lambda i: jnp.maximum(i - 1, 0)
    cur = lambda cols: pl.BlockSpec((TM, cols), lambda i: (current(i), 0))
    lag = lambda cols: pl.BlockSpec((TM, cols), lambda i: (lagged(i), 0))
    late = [w1b, w3b, w_out]
    assert all(w.shape[0] == d for w in late) and d % (n_tiles * SUBLANES_BF16) == 0
    slab = lambda w: pl.BlockSpec((d // n_tiles, w.shape[1]), lambda i: (current(i), 0))
    front = pl.pallas_call(
        functools.partial(_front_kernel, tiles_per_seq=seq // TM, n_tiles=n_tiles,
                          scale=HEAD_DIM ** -0.5 * LOG2E),
        name="front",
        grid=(n_tiles + 1,),
        in_specs=[cur(d), _resident((1, d)), _resident((d, d_ff)), _resident((d, d_ff)),
                  _resident((d_ff, d)), _resident((1, d)), _resident((d, width)),
                  _resident((d, width)), _resident((d, width)),
                  _resident((2 * width + f_rows, d)), _resident((f_rows, 1)), _resident((1, width)),
                  _resident((1, width)), _resident(ws_pairs.shape), _resident((SGU_BLOCK, width)),
                  _resident((1, width)), _resident((gw, gw))] + [slab(w) for w in late],
        out_specs=[cur(d), lag(width), cur(width), lag(LANES),
                   pl.BlockSpec((1, width, TM), lambda i: (current(i), 0, 0)),
                   pl.BlockSpec((TM // TK, width, TK), lambda i: (current(i), 0, 0))]
        + [slab(w) for w in late],
        out_shape=[jax.ShapeDtypeStruct((t_total, d), F32),
                   jax.ShapeDtypeStruct((t_total, width), BF16),
                   jax.ShapeDtypeStruct((t_total, width), BF16),
                   jax.ShapeDtypeStruct((t_total, LANES), BF16),
                   jax.ShapeDtypeStruct((t_total // TM, width, TM), BF16),
                   jax.ShapeDtypeStruct((t_total // TK, width, TK), BF16)]
        + [jax.ShapeDtypeStruct(w.shape, BF16) for w in late],
        scratch_shapes=[pltpu.VMEM((TM, d_ff), BF16), pltpu.VMEM((TM, width), F32),
                        pltpu.VMEM((f_rows, 1), F32), pltpu.VMEM((TM, 2 * width), F32),
                        pltpu.VMEM((f_rows, TM), F32)],
        compiler_params=pltpu.CompilerParams(dimension_semantics=("arbitrary",),
                                             vmem_limit_bytes=VMEM_LIMIT),
    )
    x1, ya, k, fa, qt, vt, w1b_bf, w3b_bf, wo_bf = front(
        xt, _row(g1), bf(w1a), bf(w3a), bf(w2a), _row(gm), wu, wv, wk, wqvft, bf_col,
        _row(ln_g), _row(ln_b), ws_pairs, bs_full, og[:, :width], gmat, *late)

    n_batch = t_total // seq
    nq = seq // TQ
    fox = pl.pallas_call(
        functools.partial(_fox_kernel, n_heads=n_heads),
        name="fox",
        grid=(n_batch, nq),
        in_specs=[pl.BlockSpec((TQ // TM, width, TM), lambda b, q: (b * nq + q, 0, 0)),
                  pl.BlockSpec((seq, width), lambda b, q: (b, 0)),
                  pl.BlockSpec((seq, LANES), lambda b, q: (b, 0)),
                  pl.BlockSpec((seq // TK, width, TK), lambda b, q: (b, 0, 0)),
                  pl.BlockSpec((1, width), lambda b, q: (0, 0))],
        out_specs=pl.BlockSpec((TQ, width), lambda b, q: (b * nq + q, 0)),
        out_shape=jax.ShapeDtypeStruct((t_total, width), BF16),
        scratch_shapes=[pltpu.VMEM((n_heads, 2 * LANES, TQ), BF16),
                        pltpu.VMEM((n_heads * (TQ // TQS), 1, TQS), F32),
                        pltpu.VMEM((n_heads * (TQ // TQS), HEAD_DIM + SUBLANES_BF16, TQS), F32),
                        pltpu.VMEM((width, TQ), F32),
                        pltpu.VMEM((STAGE_RING, TK, TQS), F32)],
        compiler_params=pltpu.CompilerParams(
            dimension_semantics=("arbitrary", "arbitrary"),
            vmem_limit_bytes=VMEM_LIMIT),
    )
    yb = fox(qt, k, fa, vt, og[:, width:])

    back = pl.pallas_call(
        functools.partial(_back_kernel, final_norm=final_norm),
        name="back",
        grid=(n_tiles,),
        in_specs=[tile(d), tile(width), tile(width), _resident((2 * width, d)), _resident((1, d)),
                  _resident((d, d_ff)), _resident((d, d_ff)), _resident((d_ff, d)),
                  _resident((1, d))],
        out_specs=tile(d),
        out_shape=jax.ShapeDtypeStruct((t_total, d), F32),
        scratch_shapes=[pltpu.VMEM((TM, d_ff), BF16)],
        compiler_params=pltpu.CompilerParams(dimension_semantics=("arbitrary",),
                                             vmem_limit_bytes=VMEM_LIMIT),
    )
    return back(x1, ya, yb, wo_bf, _row(g2), w1b_bf, w3b_bf, bf(w2b), _row(gf))


def kernel(x, ffn1_norm_g, ffn1_w1, ffn1_w3, ffn1_w2, mix_norm_g, w_in, fox_f_bias, sgu_ln_g, sgu_ln_b, sgu_w_s, sgu_b_s, mix_out_g, w_out, ffn2_norm_g, ffn2_w1, ffn2_w3, ffn2_w2, final_norm_g):
    n_batch, seq, d = x.shape
    depth = ffn1_w1.shape[0]
    xt = x.reshape(n_batch * seq, d)
    for l in range(depth):
        xt = _layer(xt, seq, l == depth - 1, ffn1_norm_g[l], ffn1_w1[l], ffn1_w3[l], ffn1_w2[l],
                    mix_norm_g[l], w_in[l], fox_f_bias[l], sgu_ln_g[l], sgu_ln_b[l], sgu_w_s[l],
                    sgu_b_s[l], mix_out_g[l], w_out[l], ffn2_norm_g[l], ffn2_w1[l], ffn2_w3[l],
                    ffn2_w2[l], final_norm_g)
    return xt.reshape(n_batch, seq, d)
```

```python
import functools

import jax
import jax.numpy as jnp
from jax import lax
from jax.experimental import pallas as pl
from jax.experimental.pallas import tpu as pltpu

F32 = jnp.float32
BF16 = jnp.bfloat16

HEAD_DIM = 64
CHUNK = 64
SGU_BLOCK = 128
EPS = 1e-6
LANES = 128
FORGET_PARTS = 3

TM = 512
ROW_PARTS = 2
SUBLANES_BF16 = 16
LOG2E = 1.4426950408889634

TQ = 1024
TQS = 256
TK = 256
KV_UNROLL = 4
FF_CHUNK = 256
QK_LOOKAHEAD = 5
EXP_DELAY = 3
STAGE_RING = 6
NEG = -1e30
VMEM_LIMIT = 58 * 1024 * 1024


def _iota(shape, dim):
    return lax.broadcasted_iota(jnp.int32, shape, dim)


def _rmsnorm(x, g):
    return x * lax.rsqrt(jnp.mean(x * x, axis=-1, keepdims=True) + EPS) * g


def _log_sigmoid(x):
    return jnp.minimum(x, 0.0) - jnp.log1p(jnp.exp(-jnp.abs(x)))


def _swiglu(h, w1_ref, w3_ref, w2_ref, g_scr, filler=None):
    d_ff = w1_ref.shape[1]
    for c in range(d_ff // FF_CHUNK):
        sl = slice(c * FF_CHUNK, (c + 1) * FF_CHUNK)
        a = jnp.dot(h, w1_ref[:, sl], preferred_element_type=F32)
        b = jnp.dot(h, w3_ref[:, sl], preferred_element_type=F32)
        g_scr[:, sl] = (a * jax.nn.sigmoid(a) * b).astype(BF16)
        if filler is not None:
            next(filler, None)
    if filler is not None:
        for _ in filler:
            pass
    return jnp.dot(g_scr[...], w2_ref[...], preferred_element_type=F32)


def _front_kernel(x_ref, g1_ref, w1_ref, w3_ref, w2_ref, gm_ref, wu_ref, wv_ref, wk_ref,
                  wqvft_ref, bf_ref, lng_ref, lnb_ref, ws_ref, bs_ref, og_ref,
                  gmat_ref, late_w1_ref, late_w3_ref, late_wo_ref,
                  x1_ref, ya_ref, k_ref, fa_ref, qt_ref, vt_ref,
                  bf_w1_ref, bf_w3_ref, bf_wo_ref,
                  g_scr, ya_scr, carry_ref, z_scr, zft_scr, *, tiles_per_seq, n_tiles, scale):
    i = pl.program_id(0)
    tm = x_ref.shape[0] // ROW_PARTS
    width = wu_ref.shape[1]
    n_heads = width // HEAD_DIM

    @pl.when(i == 0)
    def _():
        z_scr[...] = jnp.zeros_like(z_scr)
        zft_scr[...] = jnp.zeros_like(zft_scr)

    @pl.when((i == 0) | (lax.rem(i + tiles_per_seq - 1, tiles_per_seq) == 0))
    def _():
        carry_ref[...] = jnp.zeros_like(carry_ref)

    pr = jnp.bitwise_and(_iota((2 * SGU_BLOCK, SGU_BLOCK), 0), SGU_BLOCK - 1)
    pc = _iota((2 * SGU_BLOCK, SGU_BLOCK), 1)
    keep = (pr // CHUNK) >= (pc // CHUNK)
    first_head = _iota((SGU_BLOCK, LANES), 1) < HEAD_DIM
    f_rows = zft_scr.shape[0]
    f_head = _iota((f_rows, tm), 0)
    f_token = _iota((f_rows, tm), 1)

    parts = range(ROW_PARTS)
    rows_of = [slice(part * tm, (part + 1) * tm) for part in parts]

    def ffn_stage(part, filler=None):
        rows = rows_of[part]
        x = x_ref[rows, :]
        h = _rmsnorm(x, g1_ref[...]).astype(BF16)
        x1 = x + 0.5 * _swiglu(h, w1_ref, w3_ref, w2_ref, g_scr.at[rows, :], filler)
        x1_ref[rows, :] = x1
        return _rmsnorm(x1, gm_ref[...]).astype(BF16)

    def proj_stage(part, h2):
        rows = rows_of[part]
        zt = lax.dot_general(wqvft_ref[...], h2, (((1,), (1,)), ((), ())),
                             preferred_element_type=F32)
        qt_ref[0, :, rows] = (zt[:width] * scale).astype(BF16)
        for t in range(tm // TK):
            vt_ref[part * (tm // TK) + t] = zt[width:2 * width, t * TK:(t + 1) * TK].astype(BF16)
        zft_scr[:, rows] = zt[2 * width:]
        k_ref[rows, :] = jnp.dot(h2, wk_ref[...], preferred_element_type=F32).astype(BF16)
        z_scr[rows, :width] = jnp.dot(h2, wu_ref[...], preferred_element_type=F32)
        z_scr[rows, width:] = jnp.dot(h2, wv_ref[...], preferred_element_type=F32)

    def gate_stage(part):
        rows = rows_of[part]
        zu = z_scr[rows, :width]
        zv = z_scr[rows, width:]
        fsum = jnp.where(f_head < n_heads, _log_sigmoid(zft_scr[:, rows] + bf_ref[...]), 0.0)
        shift = 1
        while shift < tm:
            fsum = fsum + jnp.where(f_token >= shift, pltpu.roll(fsum, shift, axis=1), 0.0)
            shift *= 2
        fsum = fsum + carry_ref[...]
        carry_ref[...] = fsum[:, tm - 1:tm]
        neg = fsum * (-LOG2E)
        hi = neg.astype(BF16)
        rem = neg - hi.astype(F32)
        mid = rem.astype(BF16)
        lo = (rem - mid.astype(F32)).astype(BF16)
        f_parts = jnp.concatenate(
            [part_t.astype(F32)[:n_heads] for part_t in (hi, mid, lo)]
            + [jnp.zeros((LANES - FORGET_PARTS * n_heads, tm), F32)], axis=0)
        fa_ref[rows, :] = f_parts.T.astype(BF16)
        yield

        u = jax.nn.gelu(zu)
        yield
        gv = jax.nn.gelu(zv)
        mu = jnp.mean(gv, axis=-1, keepdims=True)
        xc = gv - mu
        var = jnp.mean(xc * xc, axis=-1, keepdims=True)
        vn = (xc * lax.rsqrt(var + EPS) * lng_ref[...] + lnb_ref[...]).astype(BF16)
        yield
        yield
        for p in range(width // LANES):
            if p == width // LANES // 2:
                yield
            ls = slice(p * LANES, (p + 1) * LANES)
            wm = jnp.where(keep, ws_ref[p], jnp.zeros_like(ws_ref[p]))
            for blk in range(tm // SGU_BLOCK):
                rs = slice(blk * SGU_BLOCK, (blk + 1) * SGU_BLOCK)
                r = jnp.dot(wm, vn[rs, ls], preferred_element_type=F32)
                mixed = jnp.where(first_head, r[:SGU_BLOCK], r[SGU_BLOCK:]) + bs_ref[:, ls]
                ya_scr[part * tm + blk * SGU_BLOCK:part * tm + (blk + 1) * SGU_BLOCK, ls] = (
                    u[rs, ls] * mixed)
        yield

        ya = ya_scr[rows, :]
        sq = (ya * ya).astype(BF16)
        gw = gmat_ref.shape[0]
        ms = jnp.concatenate(
            [jnp.dot(sq[:, c * gw:(c + 1) * gw], gmat_ref[...], preferred_element_type=F32)
             for c in range(width // gw)], axis=1)
        ya_ref[rows, :] = (ya * lax.rsqrt(ms + EPS) * og_ref[...]).astype(BF16)

    @pl.when(i < n_tiles)
    def _():
        h2_of = {}
        for part in parts:
            h2_of[part] = ffn_stage(part, gate_stage(part))
        for part in parts:
            proj_stage(part, h2_of[part])
        for src, dst in ((late_w1_ref, bf_w1_ref), (late_w3_ref, bf_w3_ref),
                         (late_wo_ref, bf_wo_ref)):
            dst[...] = src[...].astype(BF16)

    @pl.when(i == n_tiles)
    def _():
        for part in parts:
            for _ in gate_stage(part):
                pass


def _fox_kernel(qt_ref, k_ref, fa_ref, vt_ref, og_ref, yb_ref,
                qaug_scr, m_scr, acc_scr, o_scr, stage_scr, *, n_heads):
    qi = pl.program_id(1)
    q_tiles, _, tm = qt_ref.shape
    tq = q_tiles * tm
    n_sub = tq // TQS
    blocks_per_tile = tq // TK
    sub_blocks = TQS // TK
    heads_per_group = LANES // HEAD_DIM

    rows = _iota((LANES, tm), 0)
    @pl.when((pl.program_id(0) == 0) & (qi == 0))
    def _():
        for h in range(n_heads):
            ones_part = jnp.where((rows < FORGET_PARTS * n_heads) & (lax.rem(rows, n_heads) == h),
                                  1.0, 0.0).astype(BF16)
            for t in range(q_tiles):
                qaug_scr[h, LANES:, t * tm:(t + 1) * tm] = ones_part

    for h in range(n_heads):
        g, hh = divmod(h, heads_per_group)
        for t in range(q_tiles):
            qt = qt_ref[t, g * LANES:(g + 1) * LANES, :]
            cols = slice(t * tm, (t + 1) * tm)
            qaug_scr[h, :LANES, cols] = jnp.where((rows // HEAD_DIM) == hh, qt, jnp.zeros_like(qt))
    m_scr[...] = jnp.full(m_scr.shape, NEG, F32)
    acc_scr[...] = jnp.zeros(acc_scr.shape, F32)

    ones_rows = jnp.ones((SUBLANES_BF16, TK), BF16)
    key_row = _iota((TK, TQS), 0)
    query_col = _iota((TK, TQS), 1)
    diag_masks = [key_row + d * TK <= query_col for d in range(sub_blocks)]

    def run(items):
        work = [(j, sub, mask, h) for (j, sub, mask) in items for h in range(n_heads)]
        loaded = {}

        def key_block(j, g):
            if (id(j), g) not in loaded:
                k0 = pl.multiple_of(j * TK, TK)
                loaded[(id(j), g)] = jnp.concatenate(
                    [k_ref[pl.ds(k0, TK), g * LANES:(g + 1) * LANES], fa_ref[pl.ds(k0, TK), :]],
                    axis=1)
            return loaded[(id(j), g)]

        def logits(w):
            j, sub, _, h = w
            return jnp.dot(key_block(j, h // heads_per_group),
                           qaug_scr[h, :, sub * TQS:(sub + 1) * TQS],
                           preferred_element_type=F32)

        def max_pass(i, s):
            _, sub, mask, h = work[i]
            if mask is not None:
                s = jnp.where(mask, s, NEG)
            slot = h * n_sub + sub
            m = m_scr[slot]
            m_new = jnp.maximum(m, jnp.max(s, axis=0, keepdims=True))
            m_scr[slot] = m_new
            stage_scr[i % STAGE_RING] = s
            return m_new, jnp.exp2(m - m_new)

        def exp_pass(i, m_new, alpha):
            j, sub, _, h = work[i]
            slot = h * n_sub + sub
            p = jnp.exp2(stage_scr[i % STAGE_RING] - m_new).astype(BF16)
            v_aug = jnp.concatenate([vt_ref[j, h * HEAD_DIM:(h + 1) * HEAD_DIM, :], ones_rows],
                                    axis=0)
            acc_scr[slot] = alpha * acc_scr[slot] + jnp.dot(v_aug, p, preferred_element_type=F32)

        n = len(work)
        pending = [logits(w) for w in work[:QK_LOOKAHEAD]]
        stats = {}
        for i in range(n + EXP_DELAY):
            if i + QK_LOOKAHEAD < n:
                pending.append(logits(work[i + QK_LOOKAHEAD]))
            if i < n:
                stats[i] = max_pass(i, pending.pop(0))
            if i >= EXP_DELAY:
                exp_pass(i - EXP_DELAY, *stats.pop(i - EXP_DELAY))

    first = qi * blocks_per_tile

    def body(t, carry):
        blocks = [t * KV_UNROLL + u for u in range(KV_UNROLL)]
        run([(j, sub, None) for j in blocks for sub in range(n_sub)])
        return carry

    lax.fori_loop(0, first // KV_UNROLL, body, 0)
    tail = []
    for b in range(blocks_per_tile):
        j = first + b
        for sub in range(n_sub):
            d = b - sub * sub_blocks
            if d < sub_blocks:
                tail.append((j, sub, diag_masks[d] if d >= 0 else None))
    run(tail)

    for h in range(n_heads):
        for sub in range(n_sub):
            acc = acc_scr[h * n_sub + sub]
            o = acc[:HEAD_DIM] / acc[HEAD_DIM:HEAD_DIM + 1]
            o_scr[h * HEAD_DIM:(h + 1) * HEAD_DIM, sub * TQS:(sub + 1) * TQS] = (
                o * lax.rsqrt(jnp.mean(o * o, axis=0, keepdims=True) + EPS))
    yb_ref[...] = (o_scr[...].T * og_ref[...]).astype(BF16)


def _back_kernel(x1_ref, ya_ref, yb_ref, wo_ref, g2_ref, w1_ref, w3_ref, w2_ref, gf_ref,
                 out_ref, g_scr, *, final_norm):
    tm = x1_ref.shape[0] // ROW_PARTS
    rows_of = [slice(part * tm, (part + 1) * tm) for part in range(ROW_PARTS)]
    x2_of = []
    for rows in rows_of:
        y = jnp.concatenate([ya_ref[rows, :], yb_ref[rows, :]], axis=1)
        x2_of.append(x1_ref[rows, :] + jnp.dot(y, wo_ref[...], preferred_element_type=F32))
    for rows, x2 in zip(rows_of, x2_of):
        h = _rmsnorm(x2, g2_ref[...]).astype(BF16)
        x3 = x2 + 0.5 * _swiglu(h, w1_ref, w3_ref, w2_ref, g_scr.at[rows, :])
        out_ref[rows, :] = _rmsnorm(x3, gf_ref[...]) if final_norm else x3


def _resident(shape):
    return pl.BlockSpec(shape, lambda *_: (0,) * len(shape), pipeline_mode=pl.Buffered(1))


def _row(v):
    return v.reshape(1, -1).astype(F32)


def _layer(xt, seq, final_norm, g1, w1a, w3a, w2a, gm, w_in, b_f, ln_g, ln_b, w_s, b_s, out_g,
           w_out, g2, w1b, w3b, w2b, gf):
    t_total, d = xt.shape
    d_ff = w1a.shape[1]
    n_heads = b_f.shape[0]
    width = n_heads * HEAD_DIM
    assert w_s.shape == (n_heads, SGU_BLOCK, SGU_BLOCK) and ln_g.shape == (width,)
    assert w_in.shape == (d, 5 * width + n_heads) and FORGET_PARTS * n_heads <= LANES
    assert seq % TQ == 0 and TQ % TM == 0 and TM % TQS == 0 and TM % TK == 0
    assert TQ % TQS == 0 and TQS % TK == 0 and (TQ // TK) % KV_UNROLL == 0
    assert STAGE_RING > EXP_DELAY
    assert (TM // ROW_PARTS) % SGU_BLOCK == 0 and (TM // ROW_PARTS) % TK == 0
    assert TM % SGU_BLOCK == 0 and d_ff % FF_CHUNK == 0 and width % (2 * LANES) == 0

    bf = lambda a: a.astype(BF16)
    wu, wv = bf(w_in[:, :width]), bf(w_in[:, width:2 * width])
    wq, wk = w_in[:, 2 * width:3 * width], bf(w_in[:, 3 * width:4 * width])
    wva = w_in[:, 4 * width:5 * width]
    f_rows = SUBLANES_BF16 * pl.cdiv(n_heads, SUBLANES_BF16)
    wf = jnp.pad(w_in[:, 5 * width:], ((0, 0), (0, f_rows - n_heads)))
    wqvft = bf(jnp.concatenate([wq, wva, wf], axis=1).T)
    bf_col = jnp.pad(b_f.astype(F32), (0, f_rows - n_heads)).reshape(f_rows, 1)
    ws_pairs = bf(w_s.reshape(n_heads // 2, 2 * SGU_BLOCK, SGU_BLOCK))
    bs_full = jnp.repeat(b_s.T.astype(F32), HEAD_DIM, axis=1)
    gw = 2 * LANES
    gmat = bf(jnp.where((jnp.arange(gw)[:, None] // HEAD_DIM) == (jnp.arange(gw)[None, :] // HEAD_DIM),
                        1.0 / HEAD_DIM, 0.0))
    og = _row(out_g)

    n_tiles = t_total // TM
    tile = lambda cols: pl.BlockSpec((TM, cols), lambda i: (i, 0))
    current = lambda i: jnp.minimum(i, n_tiles - 1)
    lagged = lambda i: jnp.maximum(i - 1, 0)
    cur = lambda cols: pl.BlockSpec((TM, cols), lambda i: (current(i), 0))
    lag = lambda cols: pl.BlockSpec((TM, cols), lambda i: (lagged(i), 0))
    late = [w1b, w3b, w_out]
    assert all(w.shape[0] == d for w in late) and d % (n_tiles * SUBLANES_BF16) == 0
    slab = lambda w: pl.BlockSpec((d // n_tiles, w.shape[1]), lambda i: (current(i), 0))
    front = pl.pallas_call(
        functools.partial(_front_kernel, tiles_per_seq=seq // TM, n_tiles=n_tiles,
                          scale=HEAD_DIM ** -0.5 * LOG2E),
        name="front",
        grid=(n_tiles + 1,),
        in_specs=[cur(d), _resident((1, d)), _resident((d, d_ff)), _resident((d, d_ff)),
                  _resident((d_ff, d)), _resident((1, d)), _resident((d, width)),
                  _resident((d, width)), _resident((d, width)),
                  _resident((2 * width + f_rows, d)), _resident((f_rows, 1)), _resident((1, width)),
                  _resident((1, width)), _resident(ws_pairs.shape), _resident((SGU_BLOCK, width)),
                  _resident((1, width)), _resident((gw, gw))] + [slab(w) for w in late],
        out_specs=[cur(d), lag(width), cur(width), lag(LANES),
                   pl.BlockSpec((1, width, TM), lambda i: (current(i), 0, 0)),
                   pl.BlockSpec((TM // TK, width, TK), lambda i: (current(i), 0, 0))]
        + [slab(w) for w in late],
        out_shape=[jax.ShapeDtypeStruct((t_total, d), F32),
                   jax.ShapeDtypeStruct((t_total, width), BF16),
                   jax.ShapeDtypeStruct((t_total, width), BF16),
                   jax.ShapeDtypeStruct((t_total, LANES), BF16),
                   jax.ShapeDtypeStruct((t_total // TM, width, TM), BF16),
                   jax.ShapeDtypeStruct((t_total // TK, width, TK), BF16)]
        + [jax.ShapeDtypeStruct(w.shape, BF16) for w in late],
        scratch_shapes=[pltpu.VMEM((TM, d_ff), BF16), pltpu.VMEM((TM, width), F32),
                        pltpu.VMEM((f_rows, 1), F32), pltpu.VMEM((TM, 2 * width), F32),
                        pltpu.VMEM((f_rows, TM), F32)],
        compiler_params=pltpu.CompilerParams(dimension_semantics=("arbitrary",),
                                             vmem_limit_bytes=VMEM_LIMIT),
    )
    x1, ya, k, fa, qt, vt, w1b_bf, w3b_bf, wo_bf = front(
        xt, _row(g1), bf(w1a), bf(w3a), bf(w2a), _row(gm), wu, wv, wk, wqvft, bf_col,
        _row(ln_g), _row(ln_b), ws_pairs, bs_full, og[:, :width], gmat, *late)

    n_batch = t_total // seq
    nq = seq // TQ
    fox = pl.pallas_call(
        functools.partial(_fox_kernel, n_heads=n_heads),
        name="fox",
        grid=(n_batch, nq),
        in_specs=[pl.BlockSpec((TQ // TM, width, TM), lambda b, q: (b * nq + q, 0, 0)),
                  pl.BlockSpec((seq, width), lambda b, q: (b, 0)),
                  pl.BlockSpec((seq, LANES), lambda b, q: (b, 0)),
                  pl.BlockSpec((seq // TK, width, TK), lambda b, q: (b, 0, 0)),
                  pl.BlockSpec((1, width), lambda b, q: (0, 0))],
        out_specs=pl.BlockSpec((TQ, width), lambda b, q: (b * nq + q, 0)),
        out_shape=jax.ShapeDtypeStruct((t_total, width), BF16),
        scratch_shapes=[pltpu.VMEM((n_heads, 2 * LANES, TQ), BF16),
                        pltpu.VMEM((n_heads * (TQ // TQS), 1, TQS), F32),
                        pltpu.VMEM((n_heads * (TQ // TQS), HEAD_DIM + SUBLANES_BF16, TQS), F32),
                        pltpu.VMEM((width, TQ), F32),
                        pltpu.VMEM((STAGE_RING, TK, TQS), F32)],
        compiler_params=pltpu.CompilerParams(
            dimension_semantics=("arbitrary", "arbitrary"),
            vmem_limit_bytes=VMEM_LIMIT),
    )
    yb = fox(qt, k, fa, vt, og[:, width:])

    back = pl.pallas_call(
        functools.partial(_back_kernel, final_norm=final_norm),
        name="back",
        grid=(n_tiles,),
        in_specs=[tile(d), tile(width), tile(width), _resident((2 * width, d)), _resident((1, d)),
                  _resident((d, d_ff)), _resident((d, d_ff)), _resident((d_ff, d)),
                  _resident((1, d))],
        out_specs=tile(d),
        out_shape=jax.ShapeDtypeStruct((t_total, d), F32),
        scratch_shapes=[pltpu.VMEM((TM, d_ff), BF16)],
        compiler_params=pltpu.CompilerParams(dimension_semantics=("arbitrary",),
                                             vmem_limit_bytes=VMEM_LIMIT),
    )
    return back(x1, ya, yb, wo_bf, _row(g2), w1b_bf, w3b_bf, bf(w2b), _row(gf))


def kernel(x, ffn1_norm_g, ffn1_w1, ffn1_w3, ffn1_w2, mix_norm_g, w_in, fox_f_bias, sgu_ln_g, sgu_ln_b, sgu_w_s, sgu_b_s, mix_out_g, w_out, ffn2_norm_g, ffn2_w1, ffn2_w3, ffn2_w2, final_norm_g):
    n_batch, seq, d = x.shape
    depth = ffn1_w1.shape[0]
    xt = x.reshape(n_batch * seq, d)
    for l in range(depth):
        xt = _layer(xt, seq, l == depth - 1, ffn1_norm_g[l], ffn1_w1[l], ffn1_w3[l], ffn1_w2[l],
                    mix_norm_g[l], w_in[l], fox_f_bias[l], sgu_ln_g[l], sgu_ln_b[l], sgu_w_s[l],
                    sgu_b_s[l], mix_out_g[l], w_out[l], ffn2_norm_g[l], ffn2_w1[l], ffn2_w3[l],
                    ffn2_w2[l], final_norm_g)
    return xt.reshape(n_batch, seq, d)
```
